```python
import math
import jax, jax.numpy as jnp
from jax import lax
import numpy as np

D_MODEL = 1024
BATCH = 2
SEQ = 8192
DEPTH = 1

MEM_LEN = 256
N_BRANCH = 3
CONV_WIDTH = D_MODEL // 2
CONV_TAPS = 3
SSM_WIDTH = D_MODEL // 2
SSM_GROUP = 16
SSM_GROUPS = SSM_WIDTH // SSM_GROUP
SSM_STATE = 64
XATTN_HEADS = 4
XATTN_HEAD_DIM = 128
XATTN_WIDTH = XATTN_HEADS * XATTN_HEAD_DIM
D_FF = 4 * D_MODEL
GATE_COLS = N_BRANCH * D_MODEL
IN_COLS = GATE_COLS + 3 * CONV_WIDTH + SSM_WIDTH + XATTN_WIDTH
ALPHA = (2.0 * DEPTH) ** 0.25
BETA = (8.0 * DEPTH) ** -0.25
LN_EPS = 1e-5
DT_MIN = 1e-3
DT_MAX = 1e-1

kernel_name = "hybrid_gated_conv_s5_xattn_deepnorm"


def layer_norm(x, g, b):
    xf = x.astype(jnp.float32)
    mu = jnp.mean(xf, axis=-1, keepdims=True)
    var = jnp.mean(jnp.square(xf - mu), axis=-1, keepdims=True)
    y = (xf - mu) * lax.rsqrt(var + LN_EPS) * g.astype(jnp.float32) + b.astype(jnp.float32)
    return y.astype(x.dtype)


def causal_dwconv(z, w):
    s = z.shape[1]
    zp = jnp.pad(z, ((0, 0), (CONV_TAPS - 1, 0), (0, 0)))
    y = w[0] * zp[:, 0:s]
    for k in range(1, CONV_TAPS):
        y = y + w[k] * zp[:, k:k + s]
    return y


def _complex_affine_combine(e1, e2):
    a1r, a1i, b1r, b1i = e1
    a2r, a2i, b2r, b2i = e2
    ar = a1r * a2r - a1i * a2i
    ai = a1r * a2i + a1i * a2r
    br = a2r * b1r - a2i * b1i + b2r
    bi = a2r * b1i + a2i * b1r + b2i
    return (ar, ai, br, bi)


def s5_mixer(u, lam_re, lam_im, log_dt, b_re, b_im, c_re, c_im, d_skip):
    bsz, s, _ = u.shape
    f32 = jnp.float32
    uf = u.astype(f32).reshape(bsz, s, SSM_GROUPS, SSM_GROUP)
    lr = lam_re.astype(f32)
    li = lam_im.astype(f32)
    dt = jnp.exp(log_dt.astype(f32))[:, None]
    mag = jnp.exp(lr * dt)
    abar_r = mag * jnp.cos(li * dt)
    abar_i = mag * jnp.sin(li * dt)
    den = lr * lr + li * li
    nr = abar_r - 1.0
    ni = abar_i
    kr = (nr * lr + ni * li) / den
    ki = (ni * lr - nr * li) / den
    br_ = b_re.astype(f32)
    bi_ = b_im.astype(f32)
    bbar_r = kr[..., None] * br_ - ki[..., None] * bi_
    bbar_i = kr[..., None] * bi_ + ki[..., None] * br_
    bu_r = jnp.einsum("bsgh,gph->bsgp", uf, bbar_r)
    bu_i = jnp.einsum("bsgh,gph->bsgp", uf, bbar_i)
    a_r = jnp.broadcast_to(abar_r, bu_r.shape)
    a_i = jnp.broadcast_to(abar_i, bu_r.shape)
    _, _, st_r, st_i = lax.associative_scan(_complex_affine_combine, (a_r, a_i, bu_r, bu_i), axis=1)
    y = (jnp.einsum("bsgp,ghp->bsgh", st_r, c_re.astype(f32))
         - jnp.einsum("bsgp,ghp->bsgh", st_i, c_im.astype(f32)))
    y = y.reshape(bsz, s, SSM_WIDTH) + d_skip.astype(f32) * uf.reshape(bsz, s, SSM_WIDTH)
    return y.astype(u.dtype)


def memory_cross_attention(q, mem, w_kv, w_xo):
    bsz, s, _ = q.shape
    kv = jnp.einsum("bmd,dn->bmn", mem, w_kv)
    k, v = jnp.split(kv, 2, axis=-1)
    qh = q.reshape(bsz, s, XATTN_HEADS, XATTN_HEAD_DIM)
    kh = k.reshape(bsz, -1, XATTN_HEADS, XATTN_HEAD_DIM)
    vh = v.reshape(bsz, -1, XATTN_HEADS, XATTN_HEAD_DIM)
    scores = jnp.einsum("bshd,bmhd->bhsm", qh, kh).astype(jnp.float32) * (XATTN_HEAD_DIM ** -0.5)
    probs = jax.nn.softmax(scores, axis=-1).astype(vh.dtype)
    o = jnp.einsum("bhsm,bmhd->bshd", probs, vh).reshape(bsz, s, XATTN_WIDTH)
    return jnp.einsum("bsc,cd->bsd", o, w_xo)


def setup_inputs(seed: int = 0) -> dict:
    key = jax.random.key(seed)
    ks = jax.random.split(key, 32)
    f32 = jnp.float32
    L, D = DEPTH, D_MODEL

    def nrm(k, shape, std):
        return jax.random.normal(k, shape, f32) * std

    x = jax.random.normal(ks[0], (BATCH, SEQ, D), f32)
    mem = jax.random.normal(ks[1], (BATCH, MEM_LEN, D), f32)
    w_in = nrm(ks[2], (L, D, IN_COLS), D ** -0.5)
    b_gate = nrm(ks[3], (L, GATE_COLS), 0.02)
    conv_w = nrm(ks[4], (L, CONV_TAPS, CONV_WIDTH), CONV_TAPS ** -0.5)
    w_conv_out = nrm(ks[5], (L, CONV_WIDTH, D), CONV_WIDTH ** -0.5)
    n_idx = jnp.arange(SSM_STATE, dtype=f32)
    ssm_lam_re = -0.5 + nrm(ks[6], (L, SSM_GROUPS, SSM_STATE), 0.01)
    ssm_lam_im = math.pi * n_idx + nrm(ks[7], (L, SSM_GROUPS, SSM_STATE), 0.01)
    ssm_log_dt = jax.random.uniform(ks[8], (L, SSM_GROUPS), f32, math.log(DT_MIN), math.log(DT_MAX))
    bstd = (2.0 * SSM_GROUP) ** -0.5
    ssm_b_re = nrm(ks[9], (L, SSM_GROUPS, SSM_STATE, SSM_GROUP), bstd)
    ssm_b_im = nrm(ks[10], (L, SSM_GROUPS, SSM_STATE, SSM_GROUP), bstd)
    cstd = (2.0 * SSM_STATE) ** -0.5
    ssm_c_re = nrm(ks[11], (L, SSM_GROUPS, SSM_GROUP, SSM_STATE), cstd)
    ssm_c_im = nrm(ks[12], (L, SSM_GROUPS, SSM_GROUP, SSM_STATE), cstd)
    ssm_d = nrm(ks[13], (L, SSM_WIDTH), 1.0)
    w_glu = nrm(ks[14], (L, SSM_WIDTH, 2 * D), SSM_WIDTH ** -0.5)
    w_k = nrm(ks[15], (L, D, XATTN_WIDTH), D ** -0.5)
    w_v = nrm(ks[16], (L, D, XATTN_WIDTH), BETA * D ** -0.5)
    w_kv = jnp.concatenate([w_k, w_v], axis=-1)
    w_xattn_out = nrm(ks[17], (L, XATTN_WIDTH, D), XATTN_WIDTH ** -0.5)
    w_out = nrm(ks[18], (L, D, D), BETA * D ** -0.5)
    ln1_g = 1.0 + nrm(ks[19], (L, D), 0.02)
    ln1_b = nrm(ks[20], (L, D), 0.02)
    w_up = nrm(ks[21], (L, D, D_FF), BETA * D ** -0.5)
    b_up = nrm(ks[22], (L, D_FF), 0.02)
    w_down = nrm(ks[23], (L, D_FF, D), BETA * D_FF ** -0.5)
    b_down = nrm(ks[24], (L, D), 0.02)
    ln2_g = 1.0 + nrm(ks[25], (L, D), 0.02)
    ln2_b = nrm(ks[26], (L, D), 0.02)
    return {"x": x, "mem": mem, "w_in": w_in, "b_gate": b_gate, "conv_w": conv_w,
            "w_conv_out": w_conv_out, "ssm_lam_re": ssm_lam_re, "ssm_lam_im": ssm_lam_im,
            "ssm_log_dt": ssm_log_dt, "ssm_b_re": ssm_b_re, "ssm_b_im": ssm_b_im,
            "ssm_c_re": ssm_c_re, "ssm_c_im": ssm_c_im, "ssm_d": ssm_d, "w_glu": w_glu,
            "w_kv": w_kv, "w_xattn_out": w_xattn_out, "w_out": w_out, "ln1_g": ln1_g,
            "ln1_b": ln1_b, "w_up": w_up, "b_up": b_up, "w_down": w_down, "b_down": b_down,
            "ln2_g": ln2_g, "ln2_b": ln2_b}


def reference(x, mem, w_in, b_gate, conv_w, w_conv_out, ssm_lam_re, ssm_lam_im, ssm_log_dt,
              ssm_b_re, ssm_b_im, ssm_c_re, ssm_c_im, ssm_d, w_glu, w_kv, w_xattn_out, w_out,
              ln1_g, ln1_b, w_up, b_up, w_down, b_down, ln2_g, ln2_b):
    bsz, s, d = x.shape
    splits = [GATE_COLS, GATE_COLS + 3 * CONV_WIDTH, GATE_COLS + 3 * CONV_WIDTH + SSM_WIDTH]
    for l in range(DEPTH):
        proj = jnp.einsum("bsd,dn->bsn", x, w_in[l])
        gate_pre, conv_in, u, q = jnp.split(proj, splits, axis=-1)
        gates = jax.nn.sigmoid(gate_pre + b_gate[l]).reshape(bsz, s, N_BRANCH, d)

        cb, cc, ch = jnp.split(conv_in, 3, axis=-1)
        y_a = jnp.einsum("bsc,cd->bsd", cb * causal_dwconv(cc * ch, conv_w[l]), w_conv_out[l])

        y_s = jax.nn.gelu(s5_mixer(u, ssm_lam_re[l], ssm_lam_im[l], ssm_log_dt[l], ssm_b_re[l],
                                   ssm_b_im[l], ssm_c_re[l], ssm_c_im[l], ssm_d[l]))
        glu_a, glu_b = jnp.split(jnp.einsum("bsc,cn->bsn", y_s, w_glu[l]), 2, axis=-1)
        y_b = glu_a * jax.nn.sigmoid(glu_b)

        y_c = memory_cross_attention(q, mem, w_kv[l], w_xattn_out[l])

        merged = gates[:, :, 0] * y_a + gates[:, :, 1] * y_b + gates[:, :, 2] * y_c
        x = layer_norm(ALPHA * x + jnp.einsum("bsd,de->bse", merged, w_out[l]), ln1_g[l], ln1_b[l])

        hdn = jnp.square(jax.nn.relu(jnp.einsum("bsd,df->bsf", x, w_up[l]) + b_up[l]))
        x = layer_norm(ALPHA * x + jnp.einsum("bsf,fd->bsd", hdn, w_down[l]) + b_down[l],
                       ln2_g[l], ln2_b[l])
    return x
```

```python
import functools
import math

import jax
import jax.numpy as jnp
from jax import lax
from jax.experimental import pallas as pl
from jax.experimental.pallas import tpu as pltpu

D_MODEL = 1024
MEM_LEN = 256
N_BRANCH = 3
CONV_WIDTH = 512
CONV_TAPS = 3
SSM_WIDTH = 512
SSM_GROUP = 16
SSM_GROUPS = 32
SSM_STATE = 64
XATTN_HEADS = 4
XATTN_HEAD_DIM = 128
XATTN_WIDTH = 512
D_FF = 4096
GATE_COLS = N_BRANCH * D_MODEL
DEPTH = 1
ALPHA = (2.0 * DEPTH) ** 0.25
LN_EPS = 1e-5

LANES = 128
SUBLANES = 8
PACK = SUBLANES
GROUPS_PER_TILE = LANES // SSM_GROUP
N_LANE_TILES = SSM_WIDTH // LANES
STATE_PER_TILE = GROUPS_PER_TILE * SSM_STATE
N_STATE = SSM_GROUPS * SSM_STATE
PACKED_TILE = PACK * LANES

S5_ROWS = 128
MIX_TOKENS = 512
MLP_TOKENS = 512
FF_CHUNK = 1024
VMEM_LIMIT = 56 * 1024 * 1024

_BF16 = jnp.bfloat16
_F32 = jnp.float32


def _const_spec(shape):
    zeros = (0,) * len(shape)
    return pl.BlockSpec(shape, lambda *_: zeros, pipeline_mode=pl.Buffered(1))


def _dot(a, b):
    return jnp.dot(a, b, preferred_element_type=_F32)


def _layer_norm(v, g, b):
    mu = jnp.mean(v, axis=-1, keepdims=True)
    c = v - mu
    var = jnp.mean(c * c, axis=-1, keepdims=True)
    return c * lax.rsqrt(var + LN_EPS) * g + b


def _s5_tables(lam_re, lam_im, log_dt, b_re, b_im, c_re, c_im):
    hp = lax.Precision.HIGHEST
    lr, li = lam_re.astype(_F32), lam_im.astype(_F32)
    dt = jnp.exp(log_dt.astype(_F32))[:, None]
    mag = jnp.exp(lr * dt)
    ar = mag * jnp.cos(li * dt)
    ai = mag * jnp.sin(li * dt)
    den = lr * lr + li * li
    nr, ni = ar - 1.0, ai
    kr = (nr * lr + ni * li) / den
    ki = (ni * lr - nr * li) / den
    br, bi = b_re.astype(_F32), b_im.astype(_F32)
    bbr = kr[..., None] * br - ki[..., None] * bi
    bbi = kr[..., None] * bi + ki[..., None] * br
    pr, pi = [jnp.ones_like(ar)], [jnp.zeros_like(ai)]
    for _ in range(PACK):
        pr.append(pr[-1] * ar - pi[-1] * ai)
        pi.append(pr[-2] * ai + pi[-1] * ar)
    pr, pi = jnp.stack(pr), jnp.stack(pi)
    abr = pr[:PACK, :, :, None] * bbr - pi[:PACK, :, :, None] * bbi
    abi = pr[:PACK, :, :, None] * bbi + pi[:PACK, :, :, None] * bbr
    cr, ci = c_re.astype(_F32), c_im.astype(_F32)
    eye = jnp.eye(GROUPS_PER_TILE, dtype=_F32)
    t, g = N_LANE_TILES, GROUPS_PER_TILE

    def w1_part(ab):
        ab = ab[::-1].reshape(PACK, t, g, SSM_STATE, SSM_GROUP)
        w = jnp.einsum("jtgph,gk->tjghkp", ab, eye)
        return w.reshape(t, PACKED_TILE, STATE_PER_TILE)
    w1 = jnp.concatenate([w1_part(abr), w1_part(abi)], axis=-1)

    tker = (jnp.einsum("ghp,kgpx->kghx", cr, abr, precision=hp)
            - jnp.einsum("ghp,kgpx->kghx", ci, abi, precision=hp))
    ii = jnp.arange(PACK)[:, None]
    jj = jnp.arange(PACK)[None, :]
    tau = ii - jj
    tsel = tker[jnp.clip(tau, 0, PACK - 1)]
    tsel = jnp.where((tau >= 0)[:, :, None, None, None], tsel, 0.0)
    tsel = tsel.reshape(PACK, PACK, t, g, SSM_GROUP, SSM_GROUP)
    tm = jnp.einsum("ijtghx,gk->tjgxikh", tsel, eye).reshape(t, PACKED_TILE, PACKED_TILE)

    er = cr[None] * pr[1:, :, None, :] - ci[None] * pi[1:, :, None, :]
    ei = cr[None] * pi[1:, :, None, :] + ci[None] * pr[1:, :, None, :]
    def w3_part(e):
        e = e.reshape(PACK, t, g, SSM_GROUP, SSM_STATE)
        w = jnp.einsum("itghp,gk->tgpikh", e, eye)
        return w.reshape(t, STATE_PER_TILE, PACKED_TILE)
    w3 = jnp.concatenate([w3_part(er), w3_part(-ei)], axis=1)

    a8 = jnp.stack([pr[PACK].reshape(1, N_STATE), pi[PACK].reshape(1, N_STATE)])
    return w1.astype(_BF16), tm.astype(_BF16), w3.astype(_BF16), a8


def _kv_kernel(mem_ref, w_ref, o_ref):
    o_ref[...] = _dot(mem_ref[...].astype(_BF16), w_ref[...]).astype(o_ref.dtype)


def _kv_call(mem2d, w_kv):
    rows = mem2d.shape[0]
    return pl.pallas_call(
        _kv_kernel,
        out_shape=jax.ShapeDtypeStruct((rows, 2 * XATTN_WIDTH), _BF16),
        grid=(1,),
        in_specs=[pl.BlockSpec(mem2d.shape, lambda i: (0, 0)),
                  pl.BlockSpec(w_kv.shape, lambda i: (0, 0))],
        out_specs=pl.BlockSpec((rows, 2 * XATTN_WIDTH), lambda i: (0, 0)),
        compiler_params=pltpu.CompilerParams(vmem_limit_bytes=VMEM_LIMIT),
        name="kv",
    )(mem2d, w_kv)


def _s5_kernel(steps_per_seq, x8_ref, wu_ref, w1_ref, tm_ref, w3_ref, a8_ref, d_ref,
               o_ref, u8_ref, zr_ref, zi_ref, sr_ref, si_ref, carry_ref):
    rows = x8_ref.shape[0]

    @pl.when(pl.program_id(0) % steps_per_seq == 0)
    def _():
        carry_ref[...] = jnp.zeros_like(carry_ref)

    for j in range(PACK):
        xj = x8_ref[:, j * D_MODEL:(j + 1) * D_MODEL].astype(_BF16)
        u8_ref[:, j * SSM_WIDTH:(j + 1) * SSM_WIDTH] = _dot(xj, wu_ref[...])

    def packed_lhs(t):
        return jnp.concatenate(
            [u8_ref[:, j * SSM_WIDTH + t * LANES:j * SSM_WIDTH + (t + 1) * LANES]
             for j in range(PACK)], axis=1).astype(_BF16)

    for t in range(N_LANE_TILES):
        z = _dot(packed_lhs(t), w1_ref[t])
        zr_ref[:, t * STATE_PER_TILE:(t + 1) * STATE_PER_TILE] = z[:, :STATE_PER_TILE]
        zi_ref[:, t * STATE_PER_TILE:(t + 1) * STATE_PER_TILE] = z[:, STATE_PER_TILE:]

    a_r = a8_ref[0]
    a_i = a8_ref[1]

    def step(r, carry):
        s_r, s_i = carry
        sr_ref[pl.ds(r, 1), :] = s_r
        si_ref[pl.ds(r, 1), :] = s_i
        z_r = zr_ref[pl.ds(r, 1), :]
        z_i = zi_ref[pl.ds(r, 1), :]
        return (a_r * s_r - a_i * s_i + z_r, a_r * s_i + a_i * s_r + z_i)

    s_r, s_i = lax.fori_loop(0, rows, step, (carry_ref[0], carry_ref[1]))
    carry_ref[0] = s_r
    carry_ref[1] = s_i

    for t in range(N_LANE_TILES):
        sl = slice(t * STATE_PER_TILE, (t + 1) * STATE_PER_TILE)
        sprev = jnp.concatenate([sr_ref[:, sl], si_ref[:, sl]], axis=1).astype(_BF16)
        y = _dot(packed_lhs(t), tm_ref[t]) + _dot(sprev, w3_ref[t])
        d = d_ref[:, t * LANES:(t + 1) * LANES]
        for i in range(PACK):
            col = i * SSM_WIDTH + t * LANES
            yi = y[:, i * LANES:(i + 1) * LANES] + d * u8_ref[:, col:col + LANES]
            o_ref[:, col:col + LANES] = jax.nn.gelu(yi).astype(o_ref.dtype)


def _s5_call(x8, w_u, w1, tm, w3, a8, d_skip, rows_per_seq):
    n_rows = x8.shape[0]
    steps_per_seq = rows_per_seq // S5_ROWS
    packed_w = PACK * SSM_WIDTH
    return pl.pallas_call(
        functools.partial(_s5_kernel, steps_per_seq),
        out_shape=jax.ShapeDtypeStruct((n_rows, packed_w), _BF16),
        grid=(n_rows // S5_ROWS,),
        in_specs=[pl.BlockSpec((S5_ROWS, PACK * D_MODEL), lambda i: (i, 0)),
                  _const_spec(w_u.shape), _const_spec(w1.shape), _const_spec(tm.shape),
                  _const_spec(w3.shape), _const_spec(a8.shape), _const_spec(d_skip.shape)],
        out_specs=pl.BlockSpec((S5_ROWS, packed_w), lambda i: (i, 0)),
        scratch_shapes=[pltpu.VMEM((S5_ROWS, packed_w), _F32),
                        pltpu.VMEM((S5_ROWS, N_STATE), _F32),
                        pltpu.VMEM((S5_ROWS, N_STATE), _F32),
                        pltpu.VMEM((S5_ROWS, N_STATE), _F32),
                        pltpu.VMEM((S5_ROWS, N_STATE), _F32),
                        pltpu.VMEM((2, 1, N_STATE), _F32)],
        compiler_params=pltpu.CompilerParams(dimension_semantics=("arbitrary",),
                                             vmem_limit_bytes=VMEM_LIMIT),
        name="s5",
    )(x8, w_u, w1, tm, w3, a8, d_skip)


def _mixer_kernel(steps_per_seq, x_ref, ys_ref, kv_ref, wg_ref, bg_ref, wc_ref, cw_ref,
                  wco_ref, wglu_ref, wq_ref, wxo_ref, wo_ref, g_ref, b_ref,
                  o_ref, zbuf_ref):
    tm_ = x_ref.shape[0]
    x = x_ref[...]
    xb = x.astype(_BF16)

    def gate(k):
        cols = slice(k * D_MODEL, (k + 1) * D_MODEL)
        return jax.nn.sigmoid(_dot(xb, wg_ref[:, cols]) + bg_ref[:, cols])

    @pl.when(pl.program_id(0) % steps_per_seq == 0)
    def _():
        zbuf_ref[0:SUBLANES, :] = jnp.zeros((SUBLANES, CONV_WIDTH), _F32)

    cb = _dot(xb, wc_ref[:, 0:CONV_WIDTH])
    cc = _dot(xb, wc_ref[:, CONV_WIDTH:2 * CONV_WIDTH])
    ch = _dot(xb, wc_ref[:, 2 * CONV_WIDTH:3 * CONV_WIDTH])
    z = cc * ch
    zbuf_ref[SUBLANES:SUBLANES + tm_, :] = z
    conv = (cw_ref[0:1, :] * zbuf_ref[SUBLANES - 2:SUBLANES - 2 + tm_, :]
            + cw_ref[1:2, :] * zbuf_ref[SUBLANES - 1:SUBLANES - 1 + tm_, :]
            + cw_ref[2:3, :] * z)
    zbuf_ref[0:SUBLANES, :] = zbuf_ref[tm_:tm_ + SUBLANES, :]
    y_a = _dot((cb * conv).astype(_BF16), wco_ref[...])
    merged = gate(0) * y_a

    glu = _dot(ys_ref[...], wglu_ref[...])
    merged += gate(1) * (glu[:, :D_MODEL] * jax.nn.sigmoid(glu[:, D_MODEL:]))

    q = _dot(xb, wq_ref[...]).astype(_BF16)
    heads = []
    for h in range(XATTN_HEADS):
        hs = slice(h * XATTN_HEAD_DIM, (h + 1) * XATTN_HEAD_DIM)
        k_h = kv_ref[:, hs]
        v_h = kv_ref[:, XATTN_WIDTH + h * XATTN_HEAD_DIM:XATTN_WIDTH + (h + 1) * XATTN_HEAD_DIM]
        s = lax.dot_general(q[:, hs], k_h, (((1,), (1,)), ((), ())),
                            preferred_element_type=_F32) * (XATTN_HEAD_DIM ** -0.5)
        p = jnp.exp(s - jnp.max(s, axis=-1, keepdims=True))
        den = jnp.sum(p, axis=-1, keepdims=True)
        heads.append((_dot(p.astype(_BF16), v_h) / den).astype(_BF16))
    y_c = _dot(jnp.concatenate(heads, axis=1), wxo_ref[...])
    merged += gate(2) * y_c

    v = ALPHA * x + _dot(merged.astype(_BF16), wo_ref[...])
    o_ref[...] = _layer_norm(v, g_ref[...], b_ref[...])


def _mixer_call(x2d, ys, kv, w_g, b_g, w_c, conv_w, w_co, w_glu, w_q, w_xo, w_o, ln_g, ln_b,
                seq_len):
    n = x2d.shape[0]
    steps_per_seq = seq_len // MIX_TOKENS
    consts = [w_g, b_g, w_c, conv_w, w_co, w_glu, w_q, w_xo, w_o, ln_g, ln_b]
    return pl.pallas_call(
        functools.partial(_mixer_kernel, steps_per_seq),
        out_shape=jax.ShapeDtypeStruct((n, D_MODEL), _F32),
        grid=(n // MIX_TOKENS,),
        in_specs=[pl.BlockSpec((MIX_TOKENS, D_MODEL), lambda i: (i, 0)),
                  pl.BlockSpec((MIX_TOKENS, SSM_WIDTH), lambda i: (i, 0)),
                  pl.BlockSpec((MEM_LEN, 2 * XATTN_WIDTH), lambda i: (i // steps_per_seq, 0))]
                 + [_const_spec(c.shape) for c in consts],
        out_specs=pl.BlockSpec((MIX_TOKENS, D_MODEL), lambda i: (i, 0)),
        scratch_shapes=[pltpu.VMEM((MIX_TOKENS + 2 * SUBLANES, CONV_WIDTH), _F32)],
        compiler_params=pltpu.CompilerParams(dimension_semantics=("arbitrary",),
                                             vmem_limit_bytes=VMEM_LIMIT),
        name="mixers",
    )(x2d, ys, kv, *consts)


def _mlp_kernel(x_ref, wu_ref, bu_ref, wd_ref, bd_ref, g_ref, b_ref, o_ref):
    x = x_ref[...]
    xb = x.astype(_BF16)
    acc = ALPHA * x + bd_ref[...]
    for c in range(D_FF // FF_CHUNK):
        cols = slice(c * FF_CHUNK, (c + 1) * FF_CHUNK)
        h = jnp.maximum(_dot(xb, wu_ref[:, cols]) + bu_ref[:, cols], 0.0)
        acc += _dot((h * h).astype(_BF16), wd_ref[cols, :])
    o_ref[...] = _layer_norm(acc, g_ref[...], b_ref[...])


def _mlp_call(x1, w_up, b_up, w_down, b_down, ln_g, ln_b):
    n = x1.shape[0]
    consts = [w_up, b_up, w_down, b_down, ln_g, ln_b]
    return pl.pallas_call(
        _mlp_kernel,
        out_shape=jax.ShapeDtypeStruct((n, D_MODEL), _F32),
        grid=(n // MLP_TOKENS,),
        in_specs=[pl.BlockSpec((MLP_TOKENS, D_MODEL), lambda i: (i, 0))]
                 + [_const_spec(c.shape) for c in consts],
        out_specs=pl.BlockSpec((MLP_TOKENS, D_MODEL), lambda i: (i, 0)),
        compiler_params=pltpu.CompilerParams(dimension_semantics=("arbitrary",),
                                             vmem_limit_bytes=VMEM_LIMIT),
        name="mlp",
    )(x1, *consts)


def kernel(x, mem, w_in, b_gate, conv_w, w_conv_out, ssm_lam_re, ssm_lam_im, ssm_log_dt,
           ssm_b_re, ssm_b_im, ssm_c_re, ssm_c_im, ssm_d, w_glu, w_kv, w_xattn_out, w_out,
           ln1_g, ln1_b, w_up, b_up, w_down, b_down, ln2_g, ln2_b):
    bsz, seq, d = x.shape
    assert d == D_MODEL and seq % (PACK * S5_ROWS) == 0 and seq % MIX_TOKENS == 0
    assert w_in.shape[0] == DEPTH == 1
    n = bsz * seq
    l = 0
    c0 = GATE_COLS
    c1 = c0 + 3 * CONV_WIDTH
    c2 = c1 + SSM_WIDTH
    row = lambda a: a.reshape(1, -1).astype(_F32)

    w_in_b = w_in[l].astype(_BF16)
    w_g, w_c, w_u, w_q = w_in_b[:, :c0], w_in_b[:, c0:c1], w_in_b[:, c1:c2], w_in_b[:, c2:]

    w1, tm, w3, a8 = _s5_tables(ssm_lam_re[l], ssm_lam_im[l], ssm_log_dt[l], ssm_b_re[l],
                                ssm_b_im[l], ssm_c_re[l], ssm_c_im[l])

    x2d = x.reshape(n, d)
    kv = _kv_call(mem.reshape(bsz * MEM_LEN, d), w_kv[l].astype(_BF16))
    ys8 = _s5_call(x.reshape(n // PACK, PACK * d), w_u, w1, tm, w3, a8, row(ssm_d[l]),
                   seq // PACK)
    x1 = _mixer_call(x2d, ys8.reshape(n, SSM_WIDTH), kv, w_g, row(b_gate[l]), w_c,
                     conv_w[l].astype(_F32), w_conv_out[l].astype(_BF16),
                     w_glu[l].astype(_BF16), w_q, w_xattn_out[l].astype(_BF16),
                     w_out[l].astype(_BF16), row(ln1_g[l]), row(ln1_b[l]), seq)
    out = _mlp_call(x1, w_up[l].astype(_BF16), row(b_up[l]), w_down[l].astype(_BF16),
                    row(b_down[l]), row(ln2_g[l]), row(ln2_b[l]))
    return out.reshape(bsz, seq, d)
```

```python
import functools
import math

import jax
import jax.numpy as jnp
from jax import lax
from jax.experimental import pallas as pl
from jax.experimental.pallas import tpu as pltpu

D_MODEL = 1024
MEM_LEN = 256
N_BRANCH = 3
CONV_WIDTH = 512
CONV_TAPS = 3
SSM_WIDTH = 512
SSM_GROUP = 16
SSM_GROUPS = 32
SSM_STATE = 64
XATTN_HEADS = 4
XATTN_HEAD_DIM = 128
XATTN_WIDTH = 512
D_FF = 4096
GATE_COLS = N_BRANCH * D_MODEL
DEPTH = 1
ALPHA = (2.0 * DEPTH) ** 0.25
LN_EPS = 1e-5

LANES = 128
SUBLANES = 8
PACK = SUBLANES
GROUPS_PER_TILE = LANES // SSM_GROUP
N_LANE_TILES = SSM_WIDTH // LANES
STATE_PER_TILE = GROUPS_PER_TILE * SSM_STATE
N_STATE = SSM_GROUPS * SSM_STATE
PACKED_TILE = PACK * LANES

S5_ROWS = 128
MIX_TOKENS = 512
MLP_TOKENS = 512
FF_CHUNK = 1024
VMEM_LIMIT = 56 * 1024 * 1024

_BF16 = jnp.bfloat16
_F32 = jnp.float32


def _const_spec(shape):
    zeros = (0,) * len(shape)
    return pl.BlockSpec(shape, lambda *_: zeros, pipeline_mode=pl.Buffered(1))


def _dot(a, b):
    return jnp.dot(a, b, preferred_element_type=_F32)


def _layer_norm(v, g, b):
    mu = jnp.mean(v, axis=-1, keepdims=True)
    c = v - mu
    var = jnp.mean(c * c, axis=-1, keepdims=True)
    return c * lax.rsqrt(var + LN_EPS) * g + b


def _s5_tables(lam_re, lam_im, log_dt, b_re, b_im, c_re, c_im):
    hp = lax.Precision.HIGHEST
    lr, li = lam_re.astype(_F32), lam_im.astype(_F32)
    dt = jnp.exp(log_dt.astype(_F32))[:, None]
    mag = jnp.exp(lr * dt)
    ar = mag * jnp.cos(li * dt)
    ai = mag * jnp.sin(li * dt)
    den = lr * lr + li * li
    nr, ni = ar - 1.0, ai
    kr = (nr * lr + ni * li) / den
    ki = (ni * lr - nr * li) / den
    br, bi = b_re.astype(_F32), b_im.astype(_F32)
    bbr = kr[..., None] * br - ki[..., None] * bi
    bbi = kr[..., None] * bi + ki[..., None] * br
    pr, pi = [jnp.ones_like(ar)], [jnp.zeros_like(ai)]
    for _ in range(PACK):
        pr.append(pr[-1] * ar - pi[-1] * ai)
        pi.append(pr[-2] * ai + pi[-1] * ar)
    pr, pi = jnp.stack(pr), jnp.stack(pi)
    abr = pr[:PACK, :, :, None] * bbr - pi[:PACK, :, :, None] * bbi
    abi = pr[:PACK, :, :, None] * bbi + pi[:PACK, :, :, None] * bbr
    cr, ci = c_re.astype(_F32), c_im.astype(_F32)
    eye = jnp.eye(GROUPS_PER_TILE, dtype=_F32)
    t, g = N_LANE_TILES, GROUPS_PER_TILE

    def w1_part(ab):
        ab = ab[::-1].reshape(PACK, t, g, SSM_STATE, SSM_GROUP)
        w = jnp.einsum("jtgph,gk->tjghkp", ab, eye)
        return w.reshape(t, PACKED_TILE, STATE_PER_TILE)
    w1 = jnp.concatenate([w1_part(abr), w1_part(abi)], axis=-1)

    tker = (jnp.einsum("ghp,kgpx->kghx", cr, abr, precision=hp)
            - jnp.einsum("ghp,kgpx->kghx", ci, abi, precision=hp))
    ii = jnp.arange(PACK)[:, None]
    jj = jnp.arange(PACK)[None, :]
    tau = ii - jj
    tsel = tker[jnp.clip(tau, 0, PACK - 1)]
    tsel = jnp.where((tau >= 0)[:, :, None, None, None], tsel, 0.0)
    tsel = tsel.reshape(PACK, PACK, t, g, SSM_GROUP, SSM_GROUP)
    tm = jnp.einsum("ijtghx,gk->tjgxikh", tsel, eye).reshape(t, PACKED_TILE, PACKED_TILE)

    er = cr[None] * pr[1:, :, None, :] - ci[None] * pi[1:, :, None, :]
    ei = cr[None] * pi[1:, :, None, :] + ci[None] * pr[1:, :, None, :]
    def w3_part(e):
        e = e.reshape(PACK, t, g, SSM_GROUP, SSM_STATE)
        w = jnp.einsum("itghp,gk->tgpikh", e, eye)
        return w.reshape(t, STATE_PER_TILE, PACKED_TILE)
    w3 = jnp.concatenate([w3_part(er), w3_part(-ei)], axis=1)

    a8 = jnp.stack([pr[PACK].reshape(1, N_STATE), pi[PACK].reshape(1, N_STATE)])
    return w1.astype(_BF16), tm.astype(_BF16), w3.astype(_BF16), a8


def _kv_kernel(mem_ref, w_ref, o_ref):
    o_ref[...] = _dot(mem_ref[...].astype(_BF16), w_ref[...]).astype(o_ref.dtype)


def _kv_call(mem2d, w_kv):
    rows = mem2d.shape[0]
    return pl.pallas_call(
        _kv_kernel,
        out_shape=jax.ShapeDtypeStruct((rows, 2 * XATTN_WIDTH), _BF16),
        grid=(1,),
        in_specs=[pl.BlockSpec(mem2d.shape, lambda i: (0, 0)),
                  pl.BlockSpec(w_kv.shape, lambda i: (0, 0))],
        out_specs=pl.BlockSpec((rows, 2 * XATTN_WIDTH), lambda i: (0, 0)),
        compiler_params=pltpu.CompilerParams(vmem_limit_bytes=VMEM_LIMIT),
        name="kv",
    )(mem2d, w_kv)


def _s5_kernel(steps_per_seq, x_ref, wu_ref, w1_ref, tm_ref, w3_ref, a8_ref, d_ref,
               o_ref, u_ref, y_ref, zr_ref, zi_ref, sr_ref, si_ref, carry_ref):
    rows = x_ref.shape[0] // PACK

    @pl.when(pl.program_id(0) % steps_per_seq == 0)
    def _():
        carry_ref[...] = jnp.zeros_like(carry_ref)

    u = _dot(x_ref[...].astype(_BF16), wu_ref[...])
    for t in range(N_LANE_TILES):
        u_ref[t] = u[:, t * LANES:(t + 1) * LANES]

    def packed_lhs(t):
        return jnp.concatenate(
            [u_ref[t, pl.ds(j, rows, stride=PACK), :] for j in range(PACK)],
            axis=1).astype(_BF16)

    for t in range(N_LANE_TILES):
        z = _dot(packed_lhs(t), w1_ref[t])
        zr_ref[:, t * STATE_PER_TILE:(t + 1) * STATE_PER_TILE] = z[:, :STATE_PER_TILE]
        zi_ref[:, t * STATE_PER_TILE:(t + 1) * STATE_PER_TILE] = z[:, STATE_PER_TILE:]

    a_r = a8_ref[0]
    a_i = a8_ref[1]

    def step(r, carry):
        s_r, s_i = carry
        sr_ref[pl.ds(r, 1), :] = s_r
        si_ref[pl.ds(r, 1), :] = s_i
        z_r = zr_ref[pl.ds(r, 1), :]
        z_i = zi_ref[pl.ds(r, 1), :]
        return (a_r * s_r - a_i * s_i + z_r, a_r * s_i + a_i * s_r + z_i)

    s_r, s_i = lax.fori_loop(0, rows, step, (carry_ref[0], carry_ref[1]))
    carry_ref[0] = s_r
    carry_ref[1] = s_i

    for t in range(N_LANE_TILES):
        sl = slice(t * STATE_PER_TILE, (t + 1) * STATE_PER_TILE)
        sprev = jnp.concatenate([sr_ref[:, sl], si_ref[:, sl]], axis=1).astype(_BF16)
        y = _dot(packed_lhs(t), tm_ref[t]) + _dot(sprev, w3_ref[t])
        for i in range(PACK):
            y_ref[t, pl.ds(i, rows, stride=PACK), :] = y[:, i * LANES:(i + 1) * LANES]
        cols = slice(t * LANES, (t + 1) * LANES)
        o_ref[:, cols] = jax.nn.gelu(y_ref[t] + d_ref[:, cols] * u_ref[t]).astype(o_ref.dtype)


def _s5_call(x2d, w_in_b, w1, tm, w3, a8, d_skip, seq_len):
    n = x2d.shape[0]
    tokens = S5_ROWS * PACK
    steps_per_seq = seq_len // tokens
    packed_w = PACK * SSM_WIDTH
    u_block = (GATE_COLS + 3 * CONV_WIDTH) // SSM_WIDTH
    return pl.pallas_call(
        functools.partial(_s5_kernel, steps_per_seq),
        out_shape=jax.ShapeDtypeStruct((n, SSM_WIDTH), _BF16),
        grid=(n // tokens,),
        in_specs=[pl.BlockSpec((tokens, D_MODEL), lambda i: (i, 0)),
                  pl.BlockSpec((D_MODEL, SSM_WIDTH), lambda i: (0, u_block),
                               pipeline_mode=pl.Buffered(1)),
                  _const_spec(w1.shape), _const_spec(tm.shape),
                  _const_spec(w3.shape), _const_spec(a8.shape), _const_spec(d_skip.shape)],
        out_specs=pl.BlockSpec((tokens, SSM_WIDTH), lambda i: (i, 0)),
        scratch_shapes=[pltpu.VMEM((N_LANE_TILES, tokens, LANES), _F32),
                        pltpu.VMEM((N_LANE_TILES, tokens, LANES), _F32),
                        pltpu.VMEM((S5_ROWS, N_STATE), _F32),
                        pltpu.VMEM((S5_ROWS, N_STATE), _F32),
                        pltpu.VMEM((S5_ROWS, N_STATE), _F32),
                        pltpu.VMEM((S5_ROWS, N_STATE), _F32),
                        pltpu.VMEM((2, 1, N_STATE), _F32)],
        compiler_params=pltpu.CompilerParams(dimension_semantics=("arbitrary",),
                                             vmem_limit_bytes=VMEM_LIMIT),
        name="s5",
    )(x2d, w_in_b, w1, tm, w3, a8, d_skip)


def _mixer_kernel(steps_per_seq, x_ref, ys_ref, kv_ref, wg_ref, bg_ref, wc_ref, cw_ref,
                  wco_ref, wglu_ref, wq_ref, wxo_ref, wo_ref, g_ref, b_ref,
                  o_ref, zbuf_ref):
    tm_ = x_ref.shape[0]
    x = x_ref[...]
    xb = x.astype(_BF16)

    def gate(k):
        cols = slice(k * D_MODEL, (k + 1) * D_MODEL)
        return jax.nn.sigmoid(_dot(xb, wg_ref[:, cols]) + bg_ref[:, cols])

    @pl.when(pl.program_id(0) % steps_per_seq == 0)
    def _():
        zbuf_ref[0:SUBLANES, :] = jnp.zeros((SUBLANES, CONV_WIDTH), _F32)

    cb = _dot(xb, wc_ref[:, 0:CONV_WIDTH])
    cc = _dot(xb, wc_ref[:, CONV_WIDTH:2 * CONV_WIDTH])
    ch = _dot(xb, wc_ref[:, 2 * CONV_WIDTH:3 * CONV_WIDTH])
    z = cc * ch
    zbuf_ref[SUBLANES:SUBLANES + tm_, :] = z
    conv = (cw_ref[0:1, :] * zbuf_ref[SUBLANES - 2:SUBLANES - 2 + tm_, :]
            + cw_ref[1:2, :] * zbuf_ref[SUBLANES - 1:SUBLANES - 1 + tm_, :]
            + cw_ref[2:3, :] * z)
    zbuf_ref[0:SUBLANES, :] = zbuf_ref[tm_:tm_ + SUBLANES, :]
    y_a = _dot((cb * conv).astype(_BF16), wco_ref[...])
    merged = gate(0) * y_a

    glu = _dot(ys_ref[...], wglu_ref[...])
    merged += gate(1) * (glu[:, :D_MODEL] * jax.nn.sigmoid(glu[:, D_MODEL:]))

    q = _dot(xb, wq_ref[...]).astype(_BF16)
    heads = []
    for h in range(XATTN_HEADS):
        hs = slice(h * XATTN_HEAD_DIM, (h + 1) * XATTN_HEAD_DIM)
        k_h = kv_ref[:, hs]
        v_h = kv_ref[:, XATTN_WIDTH + h * XATTN_HEAD_DIM:XATTN_WIDTH + (h + 1) * XATTN_HEAD_DIM]
        s = lax.dot_general(q[:, hs], k_h, (((1,), (1,)), ((), ())),
                            preferred_element_type=_F32) * (XATTN_HEAD_DIM ** -0.5)
        p = jnp.exp(s - jnp.max(s, axis=-1, keepdims=True))
        den = jnp.sum(p, axis=-1, keepdims=True)
        heads.append((_dot(p.astype(_BF16), v_h) / den).astype(_BF16))
    y_c = _dot(jnp.concatenate(heads, axis=1), wxo_ref[...])
    merged += gate(2) * y_c

    v = ALPHA * x + _dot(merged.astype(_BF16), wo_ref[...])
    o_ref[...] = _layer_norm(v, g_ref[...], b_ref[...])


def _col_block_spec(width, start):
    assert start % width == 0
    return pl.BlockSpec((D_MODEL, width), lambda i: (0, start // width),
                        pipeline_mode=pl.Buffered(1))


def _mixer_call(x2d, ys, kv, w_in_b, b_g, conv_w, w_co, w_glu, w_xo, w_o, ln_g, ln_b, seq_len):
    n = x2d.shape[0]
    steps_per_seq = seq_len // MIX_TOKENS
    conv_start = GATE_COLS
    q_start = GATE_COLS + 3 * CONV_WIDTH + SSM_WIDTH
    operands = [w_in_b, b_g, w_in_b, conv_w, w_co, w_glu, w_in_b, w_xo, w_o, ln_g, ln_b]
    specs = [_col_block_spec(GATE_COLS, 0), _const_spec(b_g.shape),
             _col_block_spec(3 * CONV_WIDTH, conv_start)]
    specs += [_const_spec(c.shape) for c in (conv_w, w_co, w_glu)]
    specs += [_col_block_spec(XATTN_WIDTH, q_start)]
    specs += [_const_spec(c.shape) for c in (w_xo, w_o, ln_g, ln_b)]
    return pl.pallas_call(
        functools.partial(_mixer_kernel, steps_per_seq),
        out_shape=jax.ShapeDtypeStruct((n, D_MODEL), _F32),
        grid=(n // MIX_TOKENS,),
        in_specs=[pl.BlockSpec((MIX_TOKENS, D_MODEL), lambda i: (i, 0)),
                  pl.BlockSpec((MIX_TOKENS, SSM_WIDTH), lambda i: (i, 0)),
                  pl.BlockSpec((MEM_LEN, 2 * XATTN_WIDTH), lambda i: (i // steps_per_seq, 0))]
                 + specs,
        out_specs=pl.BlockSpec((MIX_TOKENS, D_MODEL), lambda i: (i, 0)),
        scratch_shapes=[pltpu.VMEM((MIX_TOKENS + 2 * SUBLANES, CONV_WIDTH), _F32)],
        compiler_params=pltpu.CompilerParams(dimension_semantics=("arbitrary",),
                                             vmem_limit_bytes=VMEM_LIMIT),
        name="mixers",
    )(x2d, ys, kv, *operands)


def _mlp_kernel(x_ref, wu_ref, bu_ref, wd_ref, bd_ref, g_ref, b_ref, o_ref):
    x = x_ref[...]
    xb = x.astype(_BF16)
    acc = ALPHA * x + bd_ref[...]
    for c in range(D_FF // FF_CHUNK):
        cols = slice(c * FF_CHUNK, (c + 1) * FF_CHUNK)
        h = jnp.maximum(_dot(xb, wu_ref[:, cols]) + bu_ref[:, cols], 0.0)
        acc += _dot((h * h).astype(_BF16), wd_ref[cols, :])
    o_ref[...] = _layer_norm(acc, g_ref[...], b_ref[...])


def _mlp_call(x1, w_up, b_up, w_down, b_down, ln_g, ln_b):
    n = x1.shape[0]
    consts = [w_up, b_up, w_down, b_down, ln_g, ln_b]
    return pl.pallas_call(
        _mlp_kernel,
        out_shape=jax.ShapeDtypeStruct((n, D_MODEL), _F32),
        grid=(n // MLP_TOKENS,),
        in_specs=[pl.BlockSpec((MLP_TOKENS, D_MODEL), lambda i: (i, 0))]
                 + [_const_spec(c.shape) for c in consts],
        out_specs=pl.BlockSpec((MLP_TOKENS, D_MODEL), lambda i: (i, 0)),
        compiler_params=pltpu.CompilerParams(dimension_semantics=("arbitrary",),
                                             vmem_limit_bytes=VMEM_LIMIT),
        name="mlp",
    )(x1, *consts)


def kernel(x, mem, w_in, b_gate, conv_w, w_conv_out, ssm_lam_re, ssm_lam_im, ssm_log_dt,
           ssm_b_re, ssm_b_im, ssm_c_re, ssm_c_im, ssm_d, w_glu, w_kv, w_xattn_out, w_out,
           ln1_g, ln1_b, w_up, b_up, w_down, b_down, ln2_g, ln2_b):
    bsz, seq, d = x.shape
    assert d == D_MODEL and seq % (PACK * S5_ROWS) == 0 and seq % MIX_TOKENS == 0
    assert w_in.shape[0] == DEPTH == 1
    n = bsz * seq
    l = 0
    row = lambda a: a.reshape(1, -1).astype(_F32)

    w_in_b = w_in[l].astype(_BF16)
    w1, tm, w3, a8 = _s5_tables(ssm_lam_re[l], ssm_lam_im[l], ssm_log_dt[l], ssm_b_re[l],
                                ssm_b_im[l], ssm_c_re[l], ssm_c_im[l])

    x2d = x.reshape(n, d)
    kv = _kv_call(mem.reshape(bsz * MEM_LEN, d), w_kv[l].astype(_BF16))
    ys = _s5_call(x2d, w_in_b, w1, tm, w3, a8, row(ssm_d[l]), seq)
    x1 = _mixer_call(x2d, ys, kv, w_in_b, row(b_gate[l]), conv_w[l].astype(_F32),
                     w_conv_out[l].astype(_BF16), w_glu[l].astype(_BF16),
                     w_xattn_out[l].astype(_BF16), w_out[l].astype(_BF16),
                     row(ln1_g[l]), row(ln1_b[l]), seq)
    out = _mlp_call(x1, w_up[l].astype(_BF16), row(b_up[l]), w_down[l].astype(_BF16),
                    row(b_down[l]), row(ln2_g[l]), row(ln2_b[l]))
    return out.reshape(bsz, seq, d)
```

```python
import functools
import math

import jax
import jax.numpy as jnp
from jax import lax
from jax.experimental import pallas as pl
from jax.experimental.pallas import tpu as pltpu

D_MODEL = 1024
MEM_LEN = 256
N_BRANCH = 3
CONV_WIDTH = 512
CONV_TAPS = 3
SSM_WIDTH = 512
SSM_GROUP = 16
SSM_GROUPS = 32
SSM_STATE = 64
XATTN_HEADS = 4
XATTN_HEAD_DIM = 128
XATTN_WIDTH = 512
D_FF = 4096
GATE_COLS = N_BRANCH * D_MODEL
DEPTH = 1
ALPHA = (2.0 * DEPTH) ** 0.25
LN_EPS = 1e-5

LANES = 128
SUBLANES = 8
PACK = SUBLANES
GROUPS_PER_TILE = LANES // SSM_GROUP
N_LANE_TILES = SSM_WIDTH // LANES
STATE_PER_TILE = GROUPS_PER_TILE * SSM_STATE
N_STATE = SSM_GROUPS * SSM_STATE
PACKED_TILE = PACK * LANES

S5_ROWS = 128
MIX_TOKENS = 512
MLP_TOKENS = 512
FF_CHUNK = 1024
VMEM_LIMIT = 56 * 1024 * 1024

_BF16 = jnp.bfloat16
_F32 = jnp.float32


def _const_spec(shape):
    zeros = (0,) * len(shape)
    return pl.BlockSpec(shape, lambda *_: zeros, pipeline_mode=pl.Buffered(1))


def _dot(a, b):
    return jnp.dot(a, b, preferred_element_type=_F32)


def _layer_norm(v, g, b):
    mu = jnp.mean(v, axis=-1, keepdims=True)
    c = v - mu
    var = jnp.mean(c * c, axis=-1, keepdims=True)
    return c * lax.rsqrt(var + LN_EPS) * g + b


def _tables_kernel(pk_ref, bt_ref, pwc_ref, ct_ref, w1_ref, tm_ref, w3_ref):
    hp = lax.Precision.HIGHEST
    rows1 = lax.broadcasted_iota(jnp.int32, (LANES, STATE_PER_TILE), 0) // SSM_GROUP
    cols1 = lax.broadcasted_iota(jnp.int32, (LANES, STATE_PER_TILE), 1) // SSM_STATE
    same1 = rows1 == cols1
    rows3 = lax.broadcasted_iota(jnp.int32, (STATE_PER_TILE, LANES), 0) // SSM_STATE
    cols3 = lax.broadcasted_iota(jnp.int32, (STATE_PER_TILE, LANES), 1) // SSM_GROUP
    same3 = rows3 == cols3

    btr, bti = bt_ref[0], bt_ref[1]
    bbar = None
    for j in range(PACK):
        pr, pi = pk_ref[j:j + 1, :], pk_ref[PACK + j:PACK + j + 1, :]
        sre = jnp.where(same1, pr * btr - pi * bti, 0.0)
        sim = jnp.where(same1, pr * bti + pi * btr, 0.0)
        w1_ref[j * LANES:(j + 1) * LANES, :STATE_PER_TILE] = sre.astype(w1_ref.dtype)
        w1_ref[j * LANES:(j + 1) * LANES, STATE_PER_TILE:] = sim.astype(w1_ref.dtype)
        if j == PACK - 1:
            bbar = jnp.concatenate([sre, sim], axis=1)

    ctr, cti = ct_ref[0], ct_ref[1]
    tker = []
    for q in range(PACK + 1):
        pr, pi = pwc_ref[:, q:q + 1], pwc_ref[:, PACK + 1 + q:PACK + 2 + q]
        ere = jnp.where(same3, ctr * pr - cti * pi, 0.0)
        eim = jnp.where(same3, ctr * pi + cti * pr, 0.0)
        slab = jnp.concatenate([ere, -eim], axis=0)
        if q >= 1:
            w3_ref[:, (q - 1) * LANES:q * LANES] = slab.astype(w3_ref.dtype)
        if q < PACK:
            tker.append(jnp.dot(bbar, slab, precision=hp,
                                preferred_element_type=_F32).astype(tm_ref.dtype))

    zero = jnp.zeros((LANES, LANES), tm_ref.dtype)
    for j in range(PACK):
        for i in range(PACK):
            tm_ref[j * LANES:(j + 1) * LANES, i * LANES:(i + 1) * LANES] = (
                tker[i - j] if i >= j else zero)


def _s5_tables(lam_re, lam_im, log_dt, b_re, b_im, c_re, c_im):
    lr, li = lam_re.astype(_F32), lam_im.astype(_F32)
    dt = jnp.exp(log_dt.astype(_F32))[:, None]
    mag = jnp.exp(lr * dt)
    ar = mag * jnp.cos(li * dt)
    ai = mag * jnp.sin(li * dt)
    den = lr * lr + li * li
    nr, ni = ar - 1.0, ai
    kr = (nr * lr + ni * li) / den
    ki = (ni * lr - nr * li) / den
    pr, pi = [jnp.ones_like(ar)], [jnp.zeros_like(ai)]
    for _ in range(PACK):
        pr.append(pr[-1] * ar - pi[-1] * ai)
        pi.append(pr[-2] * ai + pi[-1] * ar)
    pr, pi = jnp.stack(pr), jnp.stack(pi)
    t, g = N_LANE_TILES, GROUPS_PER_TILE

    pkr = (pr[:PACK] * kr - pi[:PACK] * ki)[::-1]
    pki = (pr[:PACK] * ki + pi[:PACK] * kr)[::-1]
    pk = jnp.concatenate([pkr, pki]).reshape(2 * PACK, t, STATE_PER_TILE).transpose(1, 0, 2)
    pwc = jnp.concatenate([pr, pi]).reshape(2 * (PACK + 1), t, STATE_PER_TILE).transpose(1, 2, 0)
    b = jnp.stack([b_re, b_im]).astype(_F32).reshape(2, t, g, SSM_STATE, SSM_GROUP)
    bt = jnp.broadcast_to(b.transpose(1, 0, 2, 4, 3)[:, :, :, :, None, :],
                          (t, 2, g, SSM_GROUP, g, SSM_STATE)).reshape(t, 2, LANES, STATE_PER_TILE)
    c = jnp.stack([c_re, c_im]).astype(_F32).reshape(2, t, g, SSM_GROUP, SSM_STATE)
    ct = jnp.broadcast_to(c.transpose(1, 0, 2, 4, 3)[:, :, :, :, None, :],
                          (t, 2, g, SSM_STATE, g, SSM_GROUP)).reshape(t, 2, STATE_PER_TILE, LANES)

    tile_spec = lambda *shape: pl.BlockSpec((None,) + shape, lambda i: (i,) + (0,) * len(shape))
    mat = jax.ShapeDtypeStruct((t, PACKED_TILE, PACKED_TILE), _BF16)
    w1, tm, w3 = pl.pallas_call(
        _tables_kernel,
        out_shape=(mat, mat, mat),
        grid=(t,),
        in_specs=[tile_spec(2 * PACK, STATE_PER_TILE), tile_spec(2, LANES, STATE_PER_TILE),
                  tile_spec(STATE_PER_TILE, 2 * (PACK + 1)), tile_spec(2, STATE_PER_TILE, LANES)],
        out_specs=(tile_spec(PACKED_TILE, PACKED_TILE),) * 3,
        compiler_params=pltpu.CompilerParams(dimension_semantics=("arbitrary",),
                                             vmem_limit_bytes=VMEM_LIMIT),
        name="s5_tables",
    )(pk, bt, pwc, ct)
    a8 = jnp.stack([pr[PACK].reshape(1, N_STATE), pi[PACK].reshape(1, N_STATE)])
    return w1, tm, w3, a8


def _kv_kernel(mem_ref, w_ref, o_ref):
    o_ref[...] = _dot(mem_ref[...].astype(_BF16), w_ref[...]).astype(o_ref.dtype)


def _kv_call(mem2d, w_kv):
    rows = mem2d.shape[0]
    return pl.pallas_call(
        _kv_kernel,
        out_shape=jax.ShapeDtypeStruct((rows, 2 * XATTN_WIDTH), _BF16),
        grid=(1,),
        in_specs=[pl.BlockSpec(mem2d.shape, lambda i: (0, 0)),
                  pl.BlockSpec(w_kv.shape, lambda i: (0, 0))],
        out_specs=pl.BlockSpec((rows, 2 * XATTN_WIDTH), lambda i: (0, 0)),
        compiler_params=pltpu.CompilerParams(vmem_limit_bytes=VMEM_LIMIT),
        name="kv",
    )(mem2d, w_kv)


def _s5_kernel(steps_per_seq, x_ref, wu_ref, w1_ref, tm_ref, w3_ref, a8_ref, d_ref,
               o_ref, u_ref, y_ref, zr_ref, zi_ref, sr_ref, si_ref, carry_ref):
    rows = x_ref.shape[0] // PACK

    @pl.when(pl.program_id(0) % steps_per_seq == 0)
    def _():
        carry_ref[...] = jnp.zeros_like(carry_ref)

    u = _dot(x_ref[...].astype(_BF16), wu_ref[...])
    for t in range(N_LANE_TILES):
        u_ref[t] = u[:, t * LANES:(t + 1) * LANES]

    def packed_lhs(t):
        return jnp.concatenate(
            [u_ref[t, pl.ds(j, rows, stride=PACK), :] for j in range(PACK)],
            axis=1).astype(_BF16)

    for t in range(N_LANE_TILES):
        z = _dot(packed_lhs(t), w1_ref[t])
        zr_ref[:, t * STATE_PER_TILE:(t + 1) * STATE_PER_TILE] = z[:, :STATE_PER_TILE]
        zi_ref[:, t * STATE_PER_TILE:(t + 1) * STATE_PER_TILE] = z[:, STATE_PER_TILE:]

    a_r = a8_ref[0]
    a_i = a8_ref[1]

    def step(r, carry):
        s_r, s_i = carry
        sr_ref[pl.ds(r, 1), :] = s_r
        si_ref[pl.ds(r, 1), :] = s_i
        z_r = zr_ref[pl.ds(r, 1), :]
        z_i = zi_ref[pl.ds(r, 1), :]
        return (a_r * s_r - a_i * s_i + z_r, a_r * s_i + a_i * s_r + z_i)

    s_r, s_i = lax.fori_loop(0, rows, step, (carry_ref[0], carry_ref[1]))
    carry_ref[0] = s_r
    carry_ref[1] = s_i

    for t in range(N_LANE_TILES):
        sl = slice(t * STATE_PER_TILE, (t + 1) * STATE_PER_TILE)
        sprev = jnp.concatenate([sr_ref[:, sl], si_ref[:, sl]], axis=1).astype(_BF16)
        y = _dot(packed_lhs(t), tm_ref[t]) + _dot(sprev, w3_ref[t])
        for i in range(PACK):
            y_ref[t, pl.ds(i, rows, stride=PACK), :] = y[:, i * LANES:(i + 1) * LANES]
        cols = slice(t * LANES, (t + 1) * LANES)
        o_ref[:, cols] = jax.nn.gelu(y_ref[t] + d_ref[:, cols] * u_ref[t]).astype(o_ref.dtype)


def _s5_call(x2d, w_in_b, w1, tm, w3, a8, d_skip, seq_len):
    n = x2d.shape[0]
    tokens = S5_ROWS * PACK
    steps_per_seq = seq_len // tokens
    packed_w = PACK * SSM_WIDTH
    u_block = (GATE_COLS + 3 * CONV_WIDTH) // SSM_WIDTH
    return pl.pallas_call(
        functools.partial(_s5_kernel, steps_per_seq),
        out_shape=jax.ShapeDtypeStruct((n, SSM_WIDTH), _BF16),
        grid=(n // tokens,),
        in_specs=[pl.BlockSpec((tokens, D_MODEL), lambda i: (i, 0)),
                  pl.BlockSpec((D_MODEL, SSM_WIDTH), lambda i: (0, u_block),
                               pipeline_mode=pl.Buffered(1)),
                  _const_spec(w1.shape), _const_spec(tm.shape),
                  _const_spec(w3.shape), _const_spec(a8.shape), _const_spec(d_skip.shape)],
        out_specs=pl.BlockSpec((tokens, SSM_WIDTH), lambda i: (i, 0)),
        scratch_shapes=[pltpu.VMEM((N_LANE_TILES, tokens, LANES), _F32),
                        pltpu.VMEM((N_LANE_TILES, tokens, LANES), _F32),
                        pltpu.VMEM((S5_ROWS, N_STATE), _F32),
                        pltpu.VMEM((S5_ROWS, N_STATE), _F32),
                        pltpu.VMEM((S5_ROWS, N_STATE), _F32),
                        pltpu.VMEM((S5_ROWS, N_STATE), _F32),
                        pltpu.VMEM((2, 1, N_STATE), _F32)],
        compiler_params=pltpu.CompilerParams(dimension_semantics=("arbitrary",),
                                             vmem_limit_bytes=VMEM_LIMIT),
        name="s5",
    )(x2d, w_in_b, w1, tm, w3, a8, d_skip)


def _mixer_kernel(steps_per_seq, x_ref, ys_ref, kv_ref, wg_ref, bg_ref, wc_ref, cw_ref,
                  wco_ref, wglu_ref, wq_ref, wxo_ref, wo_ref, g_ref, b_ref,
                  o_ref, zbuf_ref):
    tm_ = x_ref.shape[0]
    x = x_ref[...]
    xb = x.astype(_BF16)

    def gate(k):
        cols = slice(k * D_MODEL, (k + 1) * D_MODEL)
        return jax.nn.sigmoid(_dot(xb, wg_ref[:, cols]) + bg_ref[:, cols])

    @pl.when(pl.program_id(0) % steps_per_seq == 0)
    def _():
        zbuf_ref[0:SUBLANES, :] = jnp.zeros((SUBLANES, CONV_WIDTH), _F32)

    cb = _dot(xb, wc_ref[:, 0:CONV_WIDTH])
    cc = _dot(xb, wc_ref[:, CONV_WIDTH:2 * CONV_WIDTH])
    ch = _dot(xb, wc_ref[:, 2 * CONV_WIDTH:3 * CONV_WIDTH])
    z = cc * ch
    zbuf_ref[SUBLANES:SUBLANES + tm_, :] = z
    conv = (cw_ref[0:1, :] * zbuf_ref[SUBLANES - 2:SUBLANES - 2 + tm_, :]
            + cw_ref[1:2, :] * zbuf_ref[SUBLANES - 1:SUBLANES - 1 + tm_, :]
            + cw_ref[2:3, :] * z)
    zbuf_ref[0:SUBLANES, :] = zbuf_ref[tm_:tm_ + SUBLANES, :]
    y_a = _dot((cb * conv).astype(_BF16), wco_ref[...])
    merged = gate(0) * y_a

    glu = _dot(ys_ref[...], wglu_ref[...])
    merged += gate(1) * (glu[:, :D_MODEL] * jax.nn.sigmoid(glu[:, D_MODEL:]))

    q = _dot(xb, wq_ref[...]).astype(_BF16)
    heads = []
    for h in range(XATTN_HEADS):
        hs = slice(h * XATTN_HEAD_DIM, (h + 1) * XATTN_HEAD_DIM)
        k_h = kv_ref[:, hs]
        v_h = kv_ref[:, XATTN_WIDTH + h * XATTN_HEAD_DIM:XATTN_WIDTH + (h + 1) * XATTN_HEAD_DIM]
        s = lax.dot_general(q[:, hs], k_h, (((1,), (1,)), ((), ())),
                            preferred_element_type=_F32) * (XATTN_HEAD_DIM ** -0.5)
        p = jnp.exp(s - jnp.max(s, axis=-1, keepdims=True))
        den = jnp.sum(p, axis=-1, keepdims=True)
        heads.append((_dot(p.astype(_BF16), v_h) / den).astype(_BF16))
    y_c = _dot(jnp.concatenate(heads, axis=1), wxo_ref[...])
    merged += gate(2) * y_c

    v = ALPHA * x + _dot(merged.astype(_BF16), wo_ref[...])
    o_ref[...] = _layer_norm(v, g_ref[...], b_ref[...])


def _col_block_spec(width, start):
    assert start % width == 0
    return pl.BlockSpec((D_MODEL, width), lambda i: (0, start // width),
                        pipeline_mode=pl.Buffered(1))


def _mixer_call(x2d, ys, kv, w_in_b, b_g, conv_w, w_co, w_glu, w_xo, w_o, ln_g, ln_b, seq_len):
    n = x2d.shape[0]
    steps_per_seq = seq_len // MIX_TOKENS
    conv_start = GATE_COLS
    q_start = GATE_COLS + 3 * CONV_WIDTH + SSM_WIDTH
    operands = [w_in_b, b_g, w_in_b, conv_w, w_co, w_glu, w_in_b, w_xo, w_o, ln_g, ln_b]
    specs = [_col_block_spec(GATE_COLS, 0), _const_spec(b_g.shape),
             _col_block_spec(3 * CONV_WIDTH, conv_start)]
    specs += [_const_spec(c.shape) for c in (conv_w, w_co, w_glu)]
    specs += [_col_block_spec(XATTN_WIDTH, q_start)]
    specs += [_const_spec(c.shape) for c in (w_xo, w_o, ln_g, ln_b)]
    return pl.pallas_call(
        functools.partial(_mixer_kernel, steps_per_seq),
        out_shape=jax.ShapeDtypeStruct((n, D_MODEL), _F32),
        grid=(n // MIX_TOKENS,),
        in_specs=[pl.BlockSpec((MIX_TOKENS, D_MODEL), lambda i: (i, 0)),
                  pl.BlockSpec((MIX_TOKENS, SSM_WIDTH), lambda i: (i, 0)),
                  pl.BlockSpec((MEM_LEN, 2 * XATTN_WIDTH), lambda i: (i // steps_per_seq, 0))]
                 + specs,
        out_specs=pl.BlockSpec((MIX_TOKENS, D_MODEL), lambda i: (i, 0)),
        scratch_shapes=[pltpu.VMEM((MIX_TOKENS + 2 * SUBLANES, CONV_WIDTH), _F32)],
        compiler_params=pltpu.CompilerParams(dimension_semantics=("arbitrary",),
                                             vmem_limit_bytes=VMEM_LIMIT),
        name="mixers",
    )(x2d, ys, kv, *operands)


def _mlp_kernel(x_ref, wu_ref, bu_ref, wd_ref, bd_ref, g_ref, b_ref, o_ref):
    x = x_ref[...]
    xb = x.astype(_BF16)
    acc = ALPHA * x + bd_ref[...]
    for c in range(D_FF // FF_CHUNK):
        cols = slice(c * FF_CHUNK, (c + 1) * FF_CHUNK)
        h = jnp.maximum(_dot(xb, wu_ref[:, cols]) + bu_ref[:, cols], 0.0)
        acc += _dot((h * h).astype(_BF16), wd_ref[cols, :])
    o_ref[...] = _layer_norm(acc, g_ref[...], b_ref[...])


def _mlp_call(x1, w_up, b_up, w_down, b_down, ln_g, ln_b):
    n = x1.shape[0]
    consts = [w_up, b_up, w_down, b_down, ln_g, ln_b]
    return pl.pallas_call(
        _mlp_kernel,
        out_shape=jax.ShapeDtypeStruct((n, D_MODEL), _F32),
        grid=(n // MLP_TOKENS,),
        in_specs=[pl.BlockSpec((MLP_TOKENS, D_MODEL), lambda i: (i, 0))]
                 + [_const_spec(c.shape) for c in consts],
        out_specs=pl.BlockSpec((MLP_TOKENS, D_MODEL), lambda i: (i, 0)),
        compiler_params=pltpu.CompilerParams(dimension_semantics=("arbitrary",),
                                             vmem_limit_bytes=VMEM_LIMIT),
        name="mlp",
    )(x1, *consts)


def kernel(x, mem, w_in, b_gate, conv_w, w_conv_out, ssm_lam_re, ssm_lam_im, ssm_log_dt,
           ssm_b_re, ssm_b_im, ssm_c_re, ssm_c_im, ssm_d, w_glu, w_kv, w_xattn_out, w_out,
           ln1_g, ln1_b, w_up, b_up, w_down, b_down, ln2_g, ln2_b):
    bsz, seq, d = x.shape
    assert d == D_MODEL and seq % (PACK * S5_ROWS) == 0 and seq % MIX_TOKENS == 0
    assert w_in.shape[0] == DEPTH == 1
    n = bsz * seq
    l = 0
    row = lambda a: a.reshape(1, -1).astype(_F32)

    w_in_b = w_in[l].astype(_BF16)
    w1, tm, w3, a8 = _s5_tables(ssm_lam_re[l], ssm_lam_im[l], ssm_log_dt[l], ssm_b_re[l],
                                ssm_b_im[l], ssm_c_re[l], ssm_c_im[l])

    x2d = x.reshape(n, d)
    kv = _kv_call(mem.reshape(bsz * MEM_LEN, d), w_kv[l].astype(_BF16))
    ys = _s5_call(x2d, w_in_b, w1, tm, w3, a8, row(ssm_d[l]), seq)
    x1 = _mixer_call(x2d, ys, kv, w_in_b, row(b_gate[l]), conv_w[l].astype(_F32),
                     w_conv_out[l].astype(_BF16), w_glu[l].astype(_BF16),
                     w_xattn_out[l].astype(_BF16), w_out[l].astype(_BF16),
                     row(ln1_g[l]), row(ln1_b[l]), seq)
    out = _mlp_call(x1, w_up[l].astype(_BF16), row(b_up[l]), w_down[l].astype(_BF16),
                    row(b_down[l]), row(ln2_g[l]), row(ln2_b[l]))
    return out.reshape(bsz, seq, d)
```

```python
import functools
import math

import jax
import jax.numpy as jnp
from jax import lax
from jax.experimental import pallas as pl
from jax.experimental.pallas import tpu as pltpu

D_MODEL = 1024
MEM_LEN = 256
N_BRANCH = 3
CONV_WIDTH = 512
CONV_TAPS = 3
SSM_WIDTH = 512
SSM_GROUP = 16
SSM_GROUPS = 32
SSM_STATE = 64
XATTN_HEADS = 4
XATTN_HEAD_DIM = 128
XATTN_WIDTH = 512
D_FF = 4096
GATE_COLS = N_BRANCH * D_MODEL
DEPTH = 1
ALPHA = (2.0 * DEPTH) ** 0.25
LN_EPS = 1e-5

LANES = 128
SUBLANES = 8
PACK = SUBLANES
GROUPS_PER_TILE = LANES // SSM_GROUP
N_LANE_TILES = SSM_WIDTH // LANES
STATE_PER_TILE = GROUPS_PER_TILE * SSM_STATE
N_STATE = SSM_GROUPS * SSM_STATE
PACKED_TILE = PACK * LANES

S5_ROWS = 128
MIX_TOKENS = 1024
MIX_SUB = 256
MLP_TOKENS = 1024
MLP_SUB = 256
FF_CHUNK = 1024
VMEM_LIMIT = 56 * 1024 * 1024

_BF16 = jnp.bfloat16
_F32 = jnp.float32


def _const_spec(shape):
    zeros = (0,) * len(shape)
    return pl.BlockSpec(shape, lambda *_: zeros, pipeline_mode=pl.Buffered(1))


def _dot(a, b):
    return jnp.dot(a, b, preferred_element_type=_F32)


def _layer_norm(v, g, b):
    mu = jnp.mean(v, axis=-1, keepdims=True)
    c = v - mu
    var = jnp.mean(c * c, axis=-1, keepdims=True)
    return c * lax.rsqrt(var + LN_EPS) * g + b


def _tables_kernel(pk_ref, bt_ref, pwc_ref, ct_ref, w1_ref, tm_ref, w3_ref):
    hp = lax.Precision.HIGHEST
    rows1 = lax.broadcasted_iota(jnp.int32, (LANES, STATE_PER_TILE), 0) // SSM_GROUP
    cols1 = lax.broadcasted_iota(jnp.int32, (LANES, STATE_PER_TILE), 1) // SSM_STATE
    same1 = rows1 == cols1
    rows3 = lax.broadcasted_iota(jnp.int32, (STATE_PER_TILE, LANES), 0) // SSM_STATE
    cols3 = lax.broadcasted_iota(jnp.int32, (STATE_PER_TILE, LANES), 1) // SSM_GROUP
    same3 = rows3 == cols3

    btr, bti = bt_ref[0], bt_ref[1]
    bbar = None
    for j in range(PACK):
        pr, pi = pk_ref[j:j + 1, :], pk_ref[PACK + j:PACK + j + 1, :]
        sre = jnp.where(same1, pr * btr - pi * bti, 0.0)
        sim = jnp.where(same1, pr * bti + pi * btr, 0.0)
        w1_ref[j * LANES:(j + 1) * LANES, :STATE_PER_TILE] = sre.astype(w1_ref.dtype)
        w1_ref[j * LANES:(j + 1) * LANES, STATE_PER_TILE:] = sim.astype(w1_ref.dtype)
        if j == PACK - 1:
            bbar = jnp.concatenate([sre, sim], axis=1)

    ctr, cti = ct_ref[0], ct_ref[1]
    tker = []
    for q in range(PACK + 1):
        pr, pi = pwc_ref[:, q:q + 1], pwc_ref[:, PACK + 1 + q:PACK + 2 + q]
        ere = jnp.where(same3, ctr * pr - cti * pi, 0.0)
        eim = jnp.where(same3, ctr * pi + cti * pr, 0.0)
        slab = jnp.concatenate([ere, -eim], axis=0)
        if q >= 1:
            w3_ref[:, (q - 1) * LANES:q * LANES] = slab.astype(w3_ref.dtype)
        if q < PACK:
            tker.append(jnp.dot(bbar, slab, precision=hp,
                                preferred_element_type=_F32).astype(tm_ref.dtype))

    zero = jnp.zeros((LANES, LANES), tm_ref.dtype)
    for j in range(PACK):
        for i in range(PACK):
            tm_ref[j * LANES:(j + 1) * LANES, i * LANES:(i + 1) * LANES] = (
                tker[i - j] if i >= j else zero)


def _s5_tables(lam_re, lam_im, log_dt, b_re, b_im, c_re, c_im):
    lr, li = lam_re.astype(_F32), lam_im.astype(_F32)
    dt = jnp.exp(log_dt.astype(_F32))[:, None]
    mag = jnp.exp(lr * dt)
    ar = mag * jnp.cos(li * dt)
    ai = mag * jnp.sin(li * dt)
    den = lr * lr + li * li
    nr, ni = ar - 1.0, ai
    kr = (nr * lr + ni * li) / den
    ki = (ni * lr - nr * li) / den
    pr, pi = [jnp.ones_like(ar)], [jnp.zeros_like(ai)]
    for _ in range(PACK):
        pr.append(pr[-1] * ar - pi[-1] * ai)
        pi.append(pr[-2] * ai + pi[-1] * ar)
    pr, pi = jnp.stack(pr), jnp.stack(pi)
    t, g = N_LANE_TILES, GROUPS_PER_TILE

    pkr = (pr[:PACK] * kr - pi[:PACK] * ki)[::-1]
    pki = (pr[:PACK] * ki + pi[:PACK] * kr)[::-1]
    pk = jnp.concatenate([pkr, pki]).reshape(2 * PACK, t, STATE_PER_TILE).transpose(1, 0, 2)
    pwc = jnp.concatenate([pr, pi]).reshape(2 * (PACK + 1), t, STATE_PER_TILE).transpose(1, 2, 0)
    b = jnp.stack([b_re, b_im]).astype(_F32).reshape(2, t, g, SSM_STATE, SSM_GROUP)
    bt = jnp.broadcast_to(b.transpose(1, 0, 2, 4, 3)[:, :, :, :, None, :],
                          (t, 2, g, SSM_GROUP, g, SSM_STATE)).reshape(t, 2, LANES, STATE_PER_TILE)
    c = jnp.stack([c_re, c_im]).astype(_F32).reshape(2, t, g, SSM_GROUP, SSM_STATE)
    ct = jnp.broadcast_to(c.transpose(1, 0, 2, 4, 3)[:, :, :, :, None, :],
                          (t, 2, g, SSM_STATE, g, SSM_GROUP)).reshape(t, 2, STATE_PER_TILE, LANES)

    tile_spec = lambda *shape: pl.BlockSpec((None,) + shape, lambda i: (i,) + (0,) * len(shape))
    mat = jax.ShapeDtypeStruct((t, PACKED_TILE, PACKED_TILE), _BF16)
    w1, tm, w3 = pl.pallas_call(
        _tables_kernel,
        out_shape=(mat, mat, mat),
        grid=(t,),
        in_specs=[tile_spec(2 * PACK, STATE_PER_TILE), tile_spec(2, LANES, STATE_PER_TILE),
                  tile_spec(STATE_PER_TILE, 2 * (PACK + 1)), tile_spec(2, STATE_PER_TILE, LANES)],
        out_specs=(tile_spec(PACKED_TILE, PACKED_TILE),) * 3,
        compiler_params=pltpu.CompilerParams(dimension_semantics=("arbitrary",),
                                             vmem_limit_bytes=VMEM_LIMIT),
        name="s5_tables",
    )(pk, bt, pwc, ct)
    a8 = jnp.stack([pr[PACK].reshape(1, N_STATE), pi[PACK].reshape(1, N_STATE)])
    return w1, tm, w3, a8


def _kv_kernel(mem_ref, w_ref, o_ref):
    o_ref[...] = _dot(mem_ref[...].astype(_BF16), w_ref[...]).astype(o_ref.dtype)


def _kv_call(mem2d, w_kv):
    rows = mem2d.shape[0]
    return pl.pallas_call(
        _kv_kernel,
        out_shape=jax.ShapeDtypeStruct((rows, 2 * XATTN_WIDTH), _BF16),
        grid=(1,),
        in_specs=[pl.BlockSpec(mem2d.shape, lambda i: (0, 0)),
                  pl.BlockSpec(w_kv.shape, lambda i: (0, 0))],
        out_specs=pl.BlockSpec((rows, 2 * XATTN_WIDTH), lambda i: (0, 0)),
        compiler_params=pltpu.CompilerParams(vmem_limit_bytes=VMEM_LIMIT),
        name="kv",
    )(mem2d, w_kv)


def _s5_kernel(steps_per_seq, x_ref, wu_ref, w1_ref, tm_ref, w3_ref, a8_ref, d_ref,
               o_ref, u_ref, y_ref, zr_ref, zi_ref, sr_ref, si_ref, carry_ref):
    rows = x_ref.shape[0] // PACK

    @pl.when(pl.program_id(0) % steps_per_seq == 0)
    def _():
        carry_ref[...] = jnp.zeros_like(carry_ref)

    u = _dot(x_ref[...].astype(_BF16), wu_ref[...])
    for t in range(N_LANE_TILES):
        u_ref[t] = u[:, t * LANES:(t + 1) * LANES]

    def packed_lhs(t):
        return jnp.concatenate(
            [u_ref[t, pl.ds(j, rows, stride=PACK), :] for j in range(PACK)],
            axis=1).astype(_BF16)

    for t in range(N_LANE_TILES):
        z = _dot(packed_lhs(t), w1_ref[t])
        zr_ref[:, t * STATE_PER_TILE:(t + 1) * STATE_PER_TILE] = z[:, :STATE_PER_TILE]
        zi_ref[:, t * STATE_PER_TILE:(t + 1) * STATE_PER_TILE] = z[:, STATE_PER_TILE:]

    a_r = a8_ref[0]
    a_i = a8_ref[1]

    def step(r, carry):
        s_r, s_i = carry
        sr_ref[pl.ds(r, 1), :] = s_r
        si_ref[pl.ds(r, 1), :] = s_i
        z_r = zr_ref[pl.ds(r, 1), :]
        z_i = zi_ref[pl.ds(r, 1), :]
        return (a_r * s_r - a_i * s_i + z_r, a_r * s_i + a_i * s_r + z_i)

    s_r, s_i = lax.fori_loop(0, rows, step, (carry_ref[0], carry_ref[1]))
    carry_ref[0] = s_r
    carry_ref[1] = s_i

    for t in range(N_LANE_TILES):
        sl = slice(t * STATE_PER_TILE, (t + 1) * STATE_PER_TILE)
        sprev = jnp.concatenate([sr_ref[:, sl], si_ref[:, sl]], axis=1).astype(_BF16)
        y = _dot(packed_lhs(t), tm_ref[t]) + _dot(sprev, w3_ref[t])
        for i in range(PACK):
            y_ref[t, pl.ds(i, rows, stride=PACK), :] = y[:, i * LANES:(i + 1) * LANES]
        cols = slice(t * LANES, (t + 1) * LANES)
        o_ref[:, cols] = jax.nn.gelu(y_ref[t] + d_ref[:, cols] * u_ref[t]).astype(o_ref.dtype)


def _s5_call(x2d, w_in_b, w1, tm, w3, a8, d_skip, seq_len):
    n = x2d.shape[0]
    tokens = S5_ROWS * PACK
    steps_per_seq = seq_len // tokens
    packed_w = PACK * SSM_WIDTH
    u_block = (GATE_COLS + 3 * CONV_WIDTH) // SSM_WIDTH
    return pl.pallas_call(
        functools.partial(_s5_kernel, steps_per_seq),
        out_shape=jax.ShapeDtypeStruct((n, SSM_WIDTH), _BF16),
        grid=(n // tokens,),
        in_specs=[pl.BlockSpec((tokens, D_MODEL), lambda i: (i, 0)),
                  pl.BlockSpec((D_MODEL, SSM_WIDTH), lambda i: (0, u_block),
                               pipeline_mode=pl.Buffered(1)),
                  _const_spec(w1.shape), _const_spec(tm.shape),
                  _const_spec(w3.shape), _const_spec(a8.shape), _const_spec(d_skip.shape)],
        out_specs=pl.BlockSpec((tokens, SSM_WIDTH), lambda i: (i, 0)),
        scratch_shapes=[pltpu.VMEM((N_LANE_TILES, tokens, LANES), _F32),
                        pltpu.VMEM((N_LANE_TILES, tokens, LANES), _F32),
                        pltpu.VMEM((S5_ROWS, N_STATE), _F32),
                        pltpu.VMEM((S5_ROWS, N_STATE), _F32),
                        pltpu.VMEM((S5_ROWS, N_STATE), _F32),
                        pltpu.VMEM((S5_ROWS, N_STATE), _F32),
                        pltpu.VMEM((2, 1, N_STATE), _F32)],
        compiler_params=pltpu.CompilerParams(dimension_semantics=("arbitrary",),
                                             vmem_limit_bytes=VMEM_LIMIT),
        name="s5",
    )(x2d, w_in_b, w1, tm, w3, a8, d_skip)


def _mixer_kernel(steps_per_seq, x_ref, ys_ref, kv_ref, wg_ref, bg_ref, wc_ref, cw_ref,
                  wco_ref, wglu_ref, wq_ref, wxo_ref, wo_ref, g_ref, b_ref,
                  o_ref, zbuf_ref):
    tm_ = x_ref.shape[0]

    @pl.when(pl.program_id(0) % steps_per_seq == 0)
    def _():
        zbuf_ref[0:SUBLANES, :] = jnp.zeros((SUBLANES, CONV_WIDTH), _F32)

    for r in range(tm_ // MIX_SUB):
        r0 = r * MIX_SUB
        rows = slice(r0, r0 + MIX_SUB)
        x = x_ref[rows, :]
        xb = x.astype(_BF16)

        def gate(k):
            cols = slice(k * D_MODEL, (k + 1) * D_MODEL)
            return jax.nn.sigmoid(_dot(xb, wg_ref[:, cols]) + bg_ref[:, cols])

        cb = _dot(xb, wc_ref[:, 0:CONV_WIDTH])
        cc = _dot(xb, wc_ref[:, CONV_WIDTH:2 * CONV_WIDTH])
        ch = _dot(xb, wc_ref[:, 2 * CONV_WIDTH:3 * CONV_WIDTH])
        z = cc * ch
        zbuf_ref[SUBLANES + r0:SUBLANES + r0 + MIX_SUB, :] = z
        conv = (cw_ref[0:1, :] * zbuf_ref[SUBLANES - 2 + r0:SUBLANES - 2 + r0 + MIX_SUB, :]
                + cw_ref[1:2, :] * zbuf_ref[SUBLANES - 1 + r0:SUBLANES - 1 + r0 + MIX_SUB, :]
                + cw_ref[2:3, :] * z)
        y_a = _dot((cb * conv).astype(_BF16), wco_ref[...])
        merged = gate(0) * y_a

        glu = _dot(ys_ref[rows, :], wglu_ref[...])
        merged += gate(1) * (glu[:, :D_MODEL] * jax.nn.sigmoid(glu[:, D_MODEL:]))

        q = _dot(xb, wq_ref[...]).astype(_BF16)
        heads = []
        for h in range(XATTN_HEADS):
            hs = slice(h * XATTN_HEAD_DIM, (h + 1) * XATTN_HEAD_DIM)
            vs = slice(XATTN_WIDTH + h * XATTN_HEAD_DIM, XATTN_WIDTH + (h + 1) * XATTN_HEAD_DIM)
            s = lax.dot_general(q[:, hs], kv_ref[:, hs], (((1,), (1,)), ((), ())),
                                preferred_element_type=_F32) * (XATTN_HEAD_DIM ** -0.5)
            p = jnp.exp(s - jnp.max(s, axis=-1, keepdims=True))
            den = jnp.sum(p, axis=-1, keepdims=True)
            heads.append((_dot(p.astype(_BF16), kv_ref[:, vs]) / den).astype(_BF16))
        y_c = _dot(jnp.concatenate(heads, axis=1), wxo_ref[...])
        merged += gate(2) * y_c

        v = ALPHA * x + _dot(merged.astype(_BF16), wo_ref[...])
        o_ref[rows, :] = _layer_norm(v, g_ref[...], b_ref[...])

    zbuf_ref[0:SUBLANES, :] = zbuf_ref[tm_:tm_ + SUBLANES, :]


def _col_block_spec(width, start):
    assert start % width == 0
    return pl.BlockSpec((D_MODEL, width), lambda i: (0, start // width),
                        pipeline_mode=pl.Buffered(1))


def _mixer_call(x2d, ys, kv, w_in_b, b_g, conv_w, w_co, w_glu, w_xo, w_o, ln_g, ln_b, seq_len):
    n = x2d.shape[0]
    steps_per_seq = seq_len // MIX_TOKENS
    conv_start = GATE_COLS
    q_start = GATE_COLS + 3 * CONV_WIDTH + SSM_WIDTH
    operands = [w_in_b, b_g, w_in_b, conv_w, w_co, w_glu, w_in_b, w_xo, w_o, ln_g, ln_b]
    specs = [_col_block_spec(GATE_COLS, 0), _const_spec(b_g.shape),
             _col_block_spec(3 * CONV_WIDTH, conv_start)]
    specs += [_const_spec(c.shape) for c in (conv_w, w_co, w_glu)]
    specs += [_col_block_spec(XATTN_WIDTH, q_start)]
    specs += [_const_spec(c.shape) for c in (w_xo, w_o, ln_g, ln_b)]
    return pl.pallas_call(
        functools.partial(_mixer_kernel, steps_per_seq),
        out_shape=jax.ShapeDtypeStruct((n, D_MODEL), _F32),
        grid=(n // MIX_TOKENS,),
        in_specs=[pl.BlockSpec((MIX_TOKENS, D_MODEL), lambda i: (i, 0)),
                  pl.BlockSpec((MIX_TOKENS, SSM_WIDTH), lambda i: (i, 0)),
                  pl.BlockSpec((MEM_LEN, 2 * XATTN_WIDTH), lambda i: (i // steps_per_seq, 0))]
                 + specs,
        out_specs=pl.BlockSpec((MIX_TOKENS, D_MODEL), lambda i: (i, 0)),
        scratch_shapes=[pltpu.VMEM((MIX_TOKENS + 2 * SUBLANES, CONV_WIDTH), _F32)],
        compiler_params=pltpu.CompilerParams(dimension_semantics=("arbitrary",),
                                             vmem_limit_bytes=VMEM_LIMIT),
        name="mixers",
    )(x2d, ys, kv, *operands)


def _mlp_kernel(x_ref, wu_ref, bu_ref, wd_ref, bd_ref, g_ref, b_ref, o_ref):
    for r in range(MLP_TOKENS // MLP_SUB):
        rows = slice(r * MLP_SUB, (r + 1) * MLP_SUB)
        x = x_ref[rows, :]
        xb = x.astype(_BF16)
        acc = ALPHA * x + bd_ref[...]
        for c in range(D_FF // FF_CHUNK):
            cols = slice(c * FF_CHUNK, (c + 1) * FF_CHUNK)
            h = jnp.maximum(_dot(xb, wu_ref[:, cols]) + bu_ref[:, cols], 0.0)
            acc += _dot((h * h).astype(_BF16), wd_ref[cols, :])
        o_ref[rows, :] = _layer_norm(acc, g_ref[...], b_ref[...])


def _mlp_call(x1, w_up, b_up, w_down, b_down, ln_g, ln_b):
    n = x1.shape[0]
    consts = [w_up, b_up, w_down, b_down, ln_g, ln_b]
    return pl.pallas_call(
        _mlp_kernel,
        out_shape=jax.ShapeDtypeStruct((n, D_MODEL), _F32),
        grid=(n // MLP_TOKENS,),
        in_specs=[pl.BlockSpec((MLP_TOKENS, D_MODEL), lambda i: (i, 0))]
                 + [_const_spec(c.shape) for c in consts],
        out_specs=pl.BlockSpec((MLP_TOKENS, D_MODEL), lambda i: (i, 0)),
        compiler_params=pltpu.CompilerParams(dimension_semantics=("arbitrary",),
                                             vmem_limit_bytes=VMEM_LIMIT),
        name="mlp",
    )(x1, *consts)


def kernel(x, mem, w_in, b_gate, conv_w, w_conv_out, ssm_lam_re, ssm_lam_im, ssm_log_dt,
           ssm_b_re, ssm_b_im, ssm_c_re, ssm_c_im, ssm_d, w_glu, w_kv, w_xattn_out, w_out,
           ln1_g, ln1_b, w_up, b_up, w_down, b_down, ln2_g, ln2_b):
    bsz, seq, d = x.shape
    assert d == D_MODEL and seq % (PACK * S5_ROWS) == 0 and seq % MIX_TOKENS == 0
    assert w_in.shape[0] == DEPTH == 1
    n = bsz * seq
    l = 0
    row = lambda a: a.reshape(1, -1).astype(_F32)

    w_in_b = w_in[l].astype(_BF16)
    w1, tm, w3, a8 = _s5_tables(ssm_lam_re[l], ssm_lam_im[l], ssm_log_dt[l], ssm_b_re[l],
                                ssm_b_im[l], ssm_c_re[l], ssm_c_im[l])

    x2d = x.reshape(n, d)
    kv = _kv_call(mem.reshape(bsz * MEM_LEN, d), w_kv[l].astype(_BF16))
    ys = _s5_call(x2d, w_in_b, w1, tm, w3, a8, row(ssm_d[l]), seq)
    x1 = _mixer_call(x2d, ys, kv, w_in_b, row(b_gate[l]), conv_w[l].astype(_F32),
                     w_conv_out[l].astype(_BF16), w_glu[l].astype(_BF16),
                     w_xattn_out[l].astype(_BF16), w_out[l].astype(_BF16),
                     row(ln1_g[l]), row(ln1_b[l]), seq)
    out = _mlp_call(x1, w_up[l].astype(_BF16), row(b_up[l]), w_down[l].astype(_BF16),
                    row(b_down[l]), row(ln2_g[l]), row(ln2_b[l]))
    return out.reshape(bsz, seq, d)
```

```python
import functools
import math

import jax
import jax.numpy as jnp
from jax import lax
from jax.experimental import pallas as pl
from jax.experimental.pallas import tpu as pltpu

D_MODEL = 1024
MEM_LEN = 256
N_BRANCH = 3
CONV_WIDTH = 512
CONV_TAPS = 3
SSM_WIDTH = 512
SSM_GROUP = 16
SSM_GROUPS = 32
SSM_STATE = 64
XATTN_HEADS = 4
XATTN_HEAD_DIM = 128
XATTN_WIDTH = 512
D_FF = 4096
GATE_COLS = N_BRANCH * D_MODEL
DEPTH = 1
ALPHA = (2.0 * DEPTH) ** 0.25
LN_EPS = 1e-5

LANES = 128
SUBLANES = 8
PACK = SUBLANES
GROUPS_PER_TILE = LANES // SSM_GROUP
N_LANE_TILES = SSM_WIDTH // LANES
STATE_PER_TILE = GROUPS_PER_TILE * SSM_STATE
N_STATE = SSM_GROUPS * SSM_STATE
PACKED_TILE = PACK * LANES

S5_ROWS = 128
MIX_TOKENS = 1024
MIX_SUB = 256
MLP_TOKENS = 1024
MLP_SUB = 256
FF_CHUNK = 1024
VMEM_LIMIT = 56 * 1024 * 1024

_BF16 = jnp.bfloat16
_F32 = jnp.float32


def _const_spec(shape):
    zeros = (0,) * len(shape)
    return pl.BlockSpec(shape, lambda *_: zeros, pipeline_mode=pl.Buffered(1))


def _dot(a, b):
    return jnp.dot(a, b, preferred_element_type=_F32)


def _layer_norm(v, g, b):
    mu = jnp.mean(v, axis=-1, keepdims=True)
    c = v - mu
    var = jnp.mean(c * c, axis=-1, keepdims=True)
    return c * lax.rsqrt(var + LN_EPS) * g + b


def _build_tables(pk_ref, bt_ref, pwc_ref, ct_ref, w1_ref, tm_ref, w3_ref):
    hp = lax.Precision.HIGHEST
    rows1 = lax.broadcasted_iota(jnp.int32, (LANES, STATE_PER_TILE), 0) // SSM_GROUP
    cols1 = lax.broadcasted_iota(jnp.int32, (LANES, STATE_PER_TILE), 1) // SSM_STATE
    same1 = rows1 == cols1
    rows3 = lax.broadcasted_iota(jnp.int32, (STATE_PER_TILE, LANES), 0) // SSM_STATE
    cols3 = lax.broadcasted_iota(jnp.int32, (STATE_PER_TILE, LANES), 1) // SSM_GROUP
    same3 = rows3 == cols3

    btr, bti = bt_ref[0], bt_ref[1]
    bbar = None
    for j in range(PACK):
        pr, pi = pk_ref[j:j + 1, :], pk_ref[PACK + j:PACK + j + 1, :]
        sre = jnp.where(same1, pr * btr - pi * bti, 0.0)
        sim = jnp.where(same1, pr * bti + pi * btr, 0.0)
        w1_ref[j * LANES:(j + 1) * LANES, :STATE_PER_TILE] = sre.astype(w1_ref.dtype)
        w1_ref[j * LANES:(j + 1) * LANES, STATE_PER_TILE:] = sim.astype(w1_ref.dtype)
        if j == PACK - 1:
            bbar = jnp.concatenate([sre, sim], axis=1)

    ctr, cti = ct_ref[0], ct_ref[1]
    tker = []
    for q in range(PACK + 1):
        pr, pi = pwc_ref[:, q:q + 1], pwc_ref[:, PACK + 1 + q:PACK + 2 + q]
        ere = jnp.where(same3, ctr * pr - cti * pi, 0.0)
        eim = jnp.where(same3, ctr * pi + cti * pr, 0.0)
        slab = jnp.concatenate([ere, -eim], axis=0)
        if q >= 1:
            w3_ref[:, (q - 1) * LANES:q * LANES] = slab.astype(w3_ref.dtype)
        if q < PACK:
            tker.append(jnp.dot(bbar, slab, precision=hp,
                                preferred_element_type=_F32).astype(tm_ref.dtype))

    zero = jnp.zeros((LANES, LANES), tm_ref.dtype)
    for j in range(PACK):
        for i in range(PACK):
            tm_ref[j * LANES:(j + 1) * LANES, i * LANES:(i + 1) * LANES] = (
                tker[i - j] if i >= j else zero)


def _s5_prep(lam_re, lam_im, log_dt, b_re, b_im, c_re, c_im):
    lr, li = lam_re.astype(_F32), lam_im.astype(_F32)
    dt = jnp.exp(log_dt.astype(_F32))[:, None]
    mag = jnp.exp(lr * dt)
    ar = mag * jnp.cos(li * dt)
    ai = mag * jnp.sin(li * dt)
    den = lr * lr + li * li
    nr, ni = ar - 1.0, ai
    kr = (nr * lr + ni * li) / den
    ki = (ni * lr - nr * li) / den
    pr, pi = [jnp.ones_like(ar)], [jnp.zeros_like(ai)]
    for _ in range(PACK):
        pr.append(pr[-1] * ar - pi[-1] * ai)
        pi.append(pr[-2] * ai + pi[-1] * ar)
    pr, pi = jnp.stack(pr), jnp.stack(pi)
    t, g = N_LANE_TILES, GROUPS_PER_TILE

    pkr = (pr[:PACK] * kr - pi[:PACK] * ki)[::-1]
    pki = (pr[:PACK] * ki + pi[:PACK] * kr)[::-1]
    pk = jnp.concatenate([pkr, pki]).reshape(2 * PACK, t, STATE_PER_TILE).transpose(1, 0, 2)
    pwc = jnp.concatenate([pr, pi]).reshape(2 * (PACK + 1), t, STATE_PER_TILE).transpose(1, 2, 0)
    b = jnp.stack([b_re, b_im]).astype(_F32).reshape(2, t, g, SSM_STATE, SSM_GROUP)
    bt = jnp.broadcast_to(b.transpose(1, 0, 2, 4, 3)[:, :, :, :, None, :],
                          (t, 2, g, SSM_GROUP, g, SSM_STATE)).reshape(t, 2, LANES, STATE_PER_TILE)
    c = jnp.stack([c_re, c_im]).astype(_F32).reshape(2, t, g, SSM_GROUP, SSM_STATE)
    ct = jnp.broadcast_to(c.transpose(1, 0, 2, 4, 3)[:, :, :, :, None, :],
                          (t, 2, g, SSM_STATE, g, SSM_GROUP)).reshape(t, 2, STATE_PER_TILE, LANES)

    a8 = jnp.stack([pr[PACK].reshape(1, N_STATE), pi[PACK].reshape(1, N_STATE)])
    return pk, bt, pwc, ct, a8


def _kv_kernel(mem_ref, w_ref, o_ref):
    o_ref[...] = _dot(mem_ref[...].astype(_BF16), w_ref[...]).astype(o_ref.dtype)


def _kv_call(mem2d, w_kv):
    rows = mem2d.shape[0]
    return pl.pallas_call(
        _kv_kernel,
        out_shape=jax.ShapeDtypeStruct((rows, 2 * XATTN_WIDTH), _BF16),
        grid=(1,),
        in_specs=[pl.BlockSpec(mem2d.shape, lambda i: (0, 0)),
                  pl.BlockSpec(w_kv.shape, lambda i: (0, 0))],
        out_specs=pl.BlockSpec((rows, 2 * XATTN_WIDTH), lambda i: (0, 0)),
        compiler_params=pltpu.CompilerParams(vmem_limit_bytes=VMEM_LIMIT),
        name="kv",
    )(mem2d, w_kv)


def _s5_kernel(steps_per_seq, x_ref, wu_ref, pk_ref, bt_ref, pwc_ref, ct_ref, a8_ref, d_ref,
               o_ref, w1_ref, tm_ref, w3_ref, u_ref, xp_ref, yi_ref, y_ref,
               zr_ref, zi_ref, sr_ref, si_ref, carry_ref):
    rows = x_ref.shape[0] // PACK

    @pl.when(pl.program_id(0) == 0)
    def _():
        def build(t, c):
            _build_tables(pk_ref.at[t], bt_ref.at[t], pwc_ref.at[t], ct_ref.at[t],
                          w1_ref.at[t], tm_ref.at[t], w3_ref.at[t])
            return c
        lax.fori_loop(0, N_LANE_TILES, build, 0)

    @pl.when(pl.program_id(0) % steps_per_seq == 0)
    def _():
        carry_ref[...] = jnp.zeros_like(carry_ref)

    u = _dot(x_ref[...].astype(_BF16), wu_ref[...])
    for t in range(N_LANE_TILES):
        u_ref[t] = u[:, t * LANES:(t + 1) * LANES]

    for t in range(N_LANE_TILES):
        xp_ref[t] = jnp.concatenate(
            [u_ref[t, pl.ds(j, rows, stride=PACK), :] for j in range(PACK)],
            axis=1).astype(_BF16)
        z = _dot(xp_ref[t], w1_ref[t])
        zr_ref[:, t * STATE_PER_TILE:(t + 1) * STATE_PER_TILE] = z[:, :STATE_PER_TILE]
        zi_ref[:, t * STATE_PER_TILE:(t + 1) * STATE_PER_TILE] = z[:, STATE_PER_TILE:]

    a_r = a8_ref[0]
    a_i = a8_ref[1]
    s_r = carry_ref[0]
    s_i = carry_ref[1]
    seg = rows // N_LANE_TILES
    for t in range(N_LANE_TILES):
        yi_ref[t] = _dot(xp_ref[t], tm_ref[t])
        for r in range(t * seg, (t + 1) * seg):
            sr_ref[r:r + 1, :] = s_r
            si_ref[r:r + 1, :] = s_i
            z_r = zr_ref[r:r + 1, :]
            z_i = zi_ref[r:r + 1, :]
            s_r, s_i = a_r * s_r - a_i * s_i + z_r, a_r * s_i + a_i * s_r + z_i
    carry_ref[0] = s_r
    carry_ref[1] = s_i

    for t in range(N_LANE_TILES):
        sl = slice(t * STATE_PER_TILE, (t + 1) * STATE_PER_TILE)
        sprev = jnp.concatenate([sr_ref[:, sl], si_ref[:, sl]], axis=1).astype(_BF16)
        y = yi_ref[t] + _dot(sprev, w3_ref[t])
        for i in range(PACK):
            y_ref[t, pl.ds(i, rows, stride=PACK), :] = y[:, i * LANES:(i + 1) * LANES]
        cols = slice(t * LANES, (t + 1) * LANES)
        o_ref[:, cols] = jax.nn.gelu(y_ref[t] + d_ref[:, cols] * u_ref[t]).astype(o_ref.dtype)


def _s5_call(x2d, w_in_b, pk, bt, pwc, ct, a8, d_skip, seq_len):
    n = x2d.shape[0]
    tokens = S5_ROWS * PACK
    steps_per_seq = seq_len // tokens
    u_block = (GATE_COLS + 3 * CONV_WIDTH) // SSM_WIDTH
    table = pltpu.VMEM((N_LANE_TILES, PACKED_TILE, PACKED_TILE), _BF16)
    slab = pltpu.VMEM((N_LANE_TILES, tokens, LANES), _F32)
    state = pltpu.VMEM((S5_ROWS, N_STATE), _F32)
    return pl.pallas_call(
        functools.partial(_s5_kernel, steps_per_seq),
        out_shape=jax.ShapeDtypeStruct((n, SSM_WIDTH), _BF16),
        grid=(n // tokens,),
        in_specs=[pl.BlockSpec((tokens, D_MODEL), lambda i: (i, 0)),
                  pl.BlockSpec((D_MODEL, SSM_WIDTH), lambda i: (0, u_block),
                               pipeline_mode=pl.Buffered(1))]
                 + [_const_spec(c.shape) for c in (pk, bt, pwc, ct, a8, d_skip)],
        out_specs=pl.BlockSpec((tokens, SSM_WIDTH), lambda i: (i, 0)),
        scratch_shapes=[table, table, table, slab,
                        pltpu.VMEM((N_LANE_TILES, S5_ROWS, PACKED_TILE), _BF16),
                        pltpu.VMEM((N_LANE_TILES, S5_ROWS, PACKED_TILE), _F32),
                        slab, state, state, state, state,
                        pltpu.VMEM((2, 1, N_STATE), _F32)],
        compiler_params=pltpu.CompilerParams(dimension_semantics=("arbitrary",),
                                             vmem_limit_bytes=VMEM_LIMIT),
        name="s5",
    )(x2d, w_in_b, pk, bt, pwc, ct, a8, d_skip)


def _mixer_kernel(steps_per_seq, x_ref, ys_ref, kv_ref, wg_ref, bg_ref, wc_ref, cw_ref,
                  wco_ref, wglu_ref, wq_ref, wxo_ref, wo_ref, g_ref, b_ref,
                  o_ref, zbuf_ref):
    tm_ = x_ref.shape[0]

    @pl.when(pl.program_id(0) % steps_per_seq == 0)
    def _():
        zbuf_ref[0:SUBLANES, :] = jnp.zeros((SUBLANES, CONV_WIDTH), _F32)

    for r in range(tm_ // MIX_SUB):
        r0 = r * MIX_SUB
        rows = slice(r0, r0 + MIX_SUB)
        x = x_ref[rows, :]
        xb = x.astype(_BF16)

        def gate(k):
            cols = slice(k * D_MODEL, (k + 1) * D_MODEL)
            return jax.nn.sigmoid(_dot(xb, wg_ref[:, cols]) + bg_ref[:, cols])

        cb = _dot(xb, wc_ref[:, 0:CONV_WIDTH])
        cc = _dot(xb, wc_ref[:, CONV_WIDTH:2 * CONV_WIDTH])
        ch = _dot(xb, wc_ref[:, 2 * CONV_WIDTH:3 * CONV_WIDTH])
        z = cc * ch
        zbuf_ref[SUBLANES + r0:SUBLANES + r0 + MIX_SUB, :] = z
        conv = (cw_ref[0:1, :] * zbuf_ref[SUBLANES - 2 + r0:SUBLANES - 2 + r0 + MIX_SUB, :]
                + cw_ref[1:2, :] * zbuf_ref[SUBLANES - 1 + r0:SUBLANES - 1 + r0 + MIX_SUB, :]
                + cw_ref[2:3, :] * z)
        y_a = _dot((cb * conv).astype(_BF16), wco_ref[...])
        merged = gate(0) * y_a

        glu = _dot(ys_ref[rows, :], wglu_ref[...])
        merged += gate(1) * (glu[:, :D_MODEL] * jax.nn.sigmoid(glu[:, D_MODEL:]))

        q = _dot(xb, wq_ref[...]).astype(_BF16)
        heads = []
        for h in range(XATTN_HEADS):
            hs = slice(h * XATTN_HEAD_DIM, (h + 1) * XATTN_HEAD_DIM)
            vs = slice(XATTN_WIDTH + h * XATTN_HEAD_DIM, XATTN_WIDTH + (h + 1) * XATTN_HEAD_DIM)
            s = lax.dot_general(q[:, hs], kv_ref[:, hs], (((1,), (1,)), ((), ())),
                                preferred_element_type=_F32) * (XATTN_HEAD_DIM ** -0.5)
            p = jnp.exp(s - jnp.max(s, axis=-1, keepdims=True))
            den = jnp.sum(p, axis=-1, keepdims=True)
            heads.append((_dot(p.astype(_BF16), kv_ref[:, vs]) / den).astype(_BF16))
        y_c = _dot(jnp.concatenate(heads, axis=1), wxo_ref[...])
        merged += gate(2) * y_c

        v = ALPHA * x + _dot(merged.astype(_BF16), wo_ref[...])
        o_ref[rows, :] = _layer_norm(v, g_ref[...], b_ref[...])

    zbuf_ref[0:SUBLANES, :] = zbuf_ref[tm_:tm_ + SUBLANES, :]


def _col_block_spec(width, start):
    assert start % width == 0
    return pl.BlockSpec((D_MODEL, width), lambda i: (0, start // width),
                        pipeline_mode=pl.Buffered(1))


def _mixer_call(x2d, ys, kv, w_in_b, b_g, conv_w, w_co, w_glu, w_xo, w_o, ln_g, ln_b, seq_len):
    n = x2d.shape[0]
    steps_per_seq = seq_len // MIX_TOKENS
    conv_start = GATE_COLS
    q_start = GATE_COLS + 3 * CONV_WIDTH + SSM_WIDTH
    operands = [w_in_b, b_g, w_in_b, conv_w, w_co, w_glu, w_in_b, w_xo, w_o, ln_g, ln_b]
    specs = [_col_block_spec(GATE_COLS, 0), _const_spec(b_g.shape),
             _col_block_spec(3 * CONV_WIDTH, conv_start)]
    specs += [_const_spec(c.shape) for c in (conv_w, w_co, w_glu)]
    specs += [_col_block_spec(XATTN_WIDTH, q_start)]
    specs += [_const_spec(c.shape) for c in (w_xo, w_o, ln_g, ln_b)]
    return pl.pallas_call(
        functools.partial(_mixer_kernel, steps_per_seq),
        out_shape=jax.ShapeDtypeStruct((n, D_MODEL), _F32),
        grid=(n // MIX_TOKENS,),
        in_specs=[pl.BlockSpec((MIX_TOKENS, D_MODEL), lambda i: (i, 0)),
                  pl.BlockSpec((MIX_TOKENS, SSM_WIDTH), lambda i: (i, 0)),
                  pl.BlockSpec((MEM_LEN, 2 * XATTN_WIDTH), lambda i: (i // steps_per_seq, 0))]
                 + specs,
        out_specs=pl.BlockSpec((MIX_TOKENS, D_MODEL), lambda i: (i, 0)),
        scratch_shapes=[pltpu.VMEM((MIX_TOKENS + 2 * SUBLANES, CONV_WIDTH), _F32)],
        compiler_params=pltpu.CompilerParams(dimension_semantics=("arbitrary",),
                                             vmem_limit_bytes=VMEM_LIMIT),
        name="mixers",
    )(x2d, ys, kv, *operands)


def _mlp_kernel(x_ref, wu_ref, bu_ref, wd_ref, bd_ref, g_ref, b_ref, o_ref):
    for r in range(MLP_TOKENS // MLP_SUB):
        rows = slice(r * MLP_SUB, (r + 1) * MLP_SUB)
        x = x_ref[rows, :]
        xb = x.astype(_BF16)
        acc = ALPHA * x + bd_ref[...]
        for c in range(D_FF // FF_CHUNK):
            cols = slice(c * FF_CHUNK, (c + 1) * FF_CHUNK)
            h = jnp.maximum(_dot(xb, wu_ref[:, cols]) + bu_ref[:, cols], 0.0)
            acc += _dot((h * h).astype(_BF16), wd_ref[cols, :])
        o_ref[rows, :] = _layer_norm(acc, g_ref[...], b_ref[...])


def _mlp_call(x1, w_up, b_up, w_down, b_down, ln_g, ln_b):
    n = x1.shape[0]
    consts = [w_up, b_up, w_down, b_down, ln_g, ln_b]
    return pl.pallas_call(
        _mlp_kernel,
        out_shape=jax.ShapeDtypeStruct((n, D_MODEL), _F32),
        grid=(n // MLP_TOKENS,),
        in_specs=[pl.BlockSpec((MLP_TOKENS, D_MODEL), lambda i: (i, 0))]
                 + [_const_spec(c.shape) for c in consts],
        out_specs=pl.BlockSpec((MLP_TOKENS, D_MODEL), lambda i: (i, 0)),
        compiler_params=pltpu.CompilerParams(dimension_semantics=("arbitrary",),
                                             vmem_limit_bytes=VMEM_LIMIT),
        name="mlp",
    )(x1, *consts)


def kernel(x, mem, w_in, b_gate, conv_w, w_conv_out, ssm_lam_re, ssm_lam_im, ssm_log_dt,
           ssm_b_re, ssm_b_im, ssm_c_re, ssm_c_im, ssm_d, w_glu, w_kv, w_xattn_out, w_out,
           ln1_g, ln1_b, w_up, b_up, w_down, b_down, ln2_g, ln2_b):
    bsz, seq, d = x.shape
    assert d == D_MODEL and seq % (PACK * S5_ROWS) == 0 and seq % MIX_TOKENS == 0
    assert w_in.shape[0] == DEPTH == 1
    n = bsz * seq
    l = 0
    row = lambda a: a.reshape(1, -1).astype(_F32)

    w_in_b = w_in[l].astype(_BF16)
    tables = _s5_prep(ssm_lam_re[l], ssm_lam_im[l], ssm_log_dt[l], ssm_b_re[l],
                      ssm_b_im[l], ssm_c_re[l], ssm_c_im[l])

    x2d = x.reshape(n, d)
    kv = _kv_call(mem.reshape(bsz * MEM_LEN, d), w_kv[l].astype(_BF16))
    ys = _s5_call(x2d, w_in_b, *tables, row(ssm_d[l]), seq)
    x1 = _mixer_call(x2d, ys, kv, w_in_b, row(b_gate[l]), conv_w[l].astype(_F32),
                     w_conv_out[l].astype(_BF16), w_glu[l].astype(_BF16),
                     w_xattn_out[l].astype(_BF16), w_out[l].astype(_BF16),
                     row(ln1_g[l]), row(ln1_b[l]), seq)
    out = _mlp_call(x1, w_up[l].astype(_BF16), row(b_up[l]), w_down[l].astype(_BF16),
                    row(b_down[l]), row(ln2_g[l]), row(ln2_b[l]))
    return out.reshape(bsz, seq, d)
```

```python
import functools
import math

import jax
import jax.numpy as jnp
from jax import lax
from jax.experimental import pallas as pl
from jax.experimental.pallas import tpu as pltpu

D_MODEL = 1024
MEM_LEN = 256
N_BRANCH = 3
CONV_WIDTH = 512
CONV_TAPS = 3
SSM_WIDTH = 512
SSM_GROUP = 16
SSM_GROUPS = 32
SSM_STATE = 64
XATTN_HEADS = 4
XATTN_HEAD_DIM = 128
XATTN_WIDTH = 512
D_FF = 4096
GATE_COLS = N_BRANCH * D_MODEL
DEPTH = 1
ALPHA = (2.0 * DEPTH) ** 0.25
LN_EPS = 1e-5

LANES = 128
SUBLANES = 8
PACK = SUBLANES
GROUPS_PER_TILE = LANES // SSM_GROUP
N_LANE_TILES = SSM_WIDTH // LANES
STATE_PER_TILE = GROUPS_PER_TILE * SSM_STATE
N_STATE = SSM_GROUPS * SSM_STATE
PACKED_TILE = PACK * LANES

S5_ROWS = 128
MIX_TOKENS = 1024
MIX_SUB = 256
MLP_TOKENS = 1024
MLP_SUB = 256
FF_CHUNK = 1024
VMEM_LIMIT = 56 * 1024 * 1024

_BF16 = jnp.bfloat16
_F32 = jnp.float32


def _const_spec(shape):
    zeros = (0,) * len(shape)
    return pl.BlockSpec(shape, lambda *_: zeros, pipeline_mode=pl.Buffered(1))


def _dot(a, b):
    return jnp.dot(a, b, preferred_element_type=_F32)


def _layer_norm(v, g, b):
    mu = jnp.mean(v, axis=-1, keepdims=True)
    c = v - mu
    var = jnp.mean(c * c, axis=-1, keepdims=True)
    return c * lax.rsqrt(var + LN_EPS) * g + b


def _build_tables(pk_ref, bt_ref, pwc_ref, ct_ref, w1_ref, tm_ref, w3_ref):
    hp = lax.Precision.HIGHEST
    rows1 = lax.broadcasted_iota(jnp.int32, (LANES, STATE_PER_TILE), 0) // SSM_GROUP
    cols1 = lax.broadcasted_iota(jnp.int32, (LANES, STATE_PER_TILE), 1) // SSM_STATE
    same1 = rows1 == cols1
    rows3 = lax.broadcasted_iota(jnp.int32, (STATE_PER_TILE, LANES), 0) // SSM_STATE
    cols3 = lax.broadcasted_iota(jnp.int32, (STATE_PER_TILE, LANES), 1) // SSM_GROUP
    same3 = rows3 == cols3

    btr, bti = bt_ref[0], bt_ref[1]
    bbar = None
    for j in range(PACK):
        pr, pi = pk_ref[j:j + 1, :], pk_ref[PACK + j:PACK + j + 1, :]
        sre = jnp.where(same1, pr * btr - pi * bti, 0.0)
        sim = jnp.where(same1, pr * bti + pi * btr, 0.0)
        w1_ref[j * LANES:(j + 1) * LANES, :STATE_PER_TILE] = sre.astype(w1_ref.dtype)
        w1_ref[j * LANES:(j + 1) * LANES, STATE_PER_TILE:] = sim.astype(w1_ref.dtype)
        if j == PACK - 1:
            bbar = jnp.concatenate([sre, sim], axis=1)

    ctr, cti = ct_ref[0], ct_ref[1]
    tker = []
    for q in range(PACK + 1):
        pr, pi = pwc_ref[:, q:q + 1], pwc_ref[:, PACK + 1 + q:PACK + 2 + q]
        ere = jnp.where(same3, ctr * pr - cti * pi, 0.0)
        eim = jnp.where(same3, ctr * pi + cti * pr, 0.0)
        slab = jnp.concatenate([ere, -eim], axis=0)
        if q >= 1:
            w3_ref[:, (q - 1) * LANES:q * LANES] = slab.astype(w3_ref.dtype)
        if q < PACK:
            tker.append(jnp.dot(bbar, slab, precision=hp,
                                preferred_element_type=_F32).astype(tm_ref.dtype))

    zero = jnp.zeros((LANES, LANES), tm_ref.dtype)
    for j in range(PACK):
        for i in range(PACK):
            tm_ref[j * LANES:(j + 1) * LANES, i * LANES:(i + 1) * LANES] = (
                tker[i - j] if i >= j else zero)


def _s5_prep(lam_re, lam_im, log_dt, b_re, b_im, c_re, c_im):
    lr, li = lam_re.astype(_F32), lam_im.astype(_F32)
    dt = jnp.exp(log_dt.astype(_F32))[:, None]
    mag = jnp.exp(lr * dt)
    ar = mag * jnp.cos(li * dt)
    ai = mag * jnp.sin(li * dt)
    den = lr * lr + li * li
    nr, ni = ar - 1.0, ai
    kr = (nr * lr + ni * li) / den
    ki = (ni * lr - nr * li) / den
    pr, pi = [jnp.ones_like(ar)], [jnp.zeros_like(ai)]
    for _ in range(PACK):
        pr.append(pr[-1] * ar - pi[-1] * ai)
        pi.append(pr[-2] * ai + pi[-1] * ar)
    pr, pi = jnp.stack(pr), jnp.stack(pi)
    t, g = N_LANE_TILES, GROUPS_PER_TILE

    pkr = (pr[:PACK] * kr - pi[:PACK] * ki)[::-1]
    pki = (pr[:PACK] * ki + pi[:PACK] * kr)[::-1]
    pk = jnp.concatenate([pkr, pki]).reshape(2 * PACK, t, STATE_PER_TILE).transpose(1, 0, 2)
    pwc = jnp.concatenate([pr, pi]).reshape(2 * (PACK + 1), t, STATE_PER_TILE).transpose(1, 2, 0)
    b = jnp.stack([b_re, b_im]).astype(_F32).reshape(2, t, g, SSM_STATE, SSM_GROUP)
    bt = jnp.broadcast_to(b.transpose(1, 0, 2, 4, 3)[:, :, :, :, None, :],
                          (t, 2, g, SSM_GROUP, g, SSM_STATE)).reshape(t, 2, LANES, STATE_PER_TILE)
    c = jnp.stack([c_re, c_im]).astype(_F32).reshape(2, t, g, SSM_GROUP, SSM_STATE)
    ct = jnp.broadcast_to(c.transpose(1, 0, 2, 4, 3)[:, :, :, :, None, :],
                          (t, 2, g, SSM_STATE, g, SSM_GROUP)).reshape(t, 2, STATE_PER_TILE, LANES)

    a8 = jnp.stack([pr[PACK].reshape(1, N_STATE), pi[PACK].reshape(1, N_STATE)])
    return pk, bt, pwc, ct, a8


def _kv_kernel(mem_ref, w_ref, o_ref):
    o_ref[...] = _dot(mem_ref[...].astype(_BF16), w_ref[...]).astype(o_ref.dtype)


def _kv_call(mem2d, w_kv):
    rows = mem2d.shape[0]
    return pl.pallas_call(
        _kv_kernel,
        out_shape=jax.ShapeDtypeStruct((rows, 2 * XATTN_WIDTH), _BF16),
        grid=(1,),
        in_specs=[pl.BlockSpec(mem2d.shape, lambda i: (0, 0)),
                  pl.BlockSpec(w_kv.shape, lambda i: (0, 0))],
        out_specs=pl.BlockSpec((rows, 2 * XATTN_WIDTH), lambda i: (0, 0)),
        compiler_params=pltpu.CompilerParams(vmem_limit_bytes=VMEM_LIMIT),
        name="kv",
    )(mem2d, w_kv)


def _s5_kernel(steps_per_seq, x_ref, wu_ref, pk_ref, bt_ref, pwc_ref, ct_ref, a8_ref, d_ref,
               o_ref, w1_ref, tm_ref, w3_ref, u_ref, xp_ref, yi_ref, y_ref,
               zr_ref, zi_ref, sr_ref, si_ref, carry_ref):
    rows = x_ref.shape[0] // PACK

    @pl.when(pl.program_id(0) == 0)
    def _():
        def build(t, c):
            _build_tables(pk_ref.at[t], bt_ref.at[t], pwc_ref.at[t], ct_ref.at[t],
                          w1_ref.at[t], tm_ref.at[t], w3_ref.at[t])
            return c
        lax.fori_loop(0, N_LANE_TILES, build, 0)

    @pl.when(pl.program_id(0) % steps_per_seq == 0)
    def _():
        carry_ref[...] = jnp.zeros_like(carry_ref)

    u = _dot(x_ref[...].astype(_BF16), wu_ref[...])
    for t in range(N_LANE_TILES):
        u_ref[t] = u[:, t * LANES:(t + 1) * LANES]

    for t in range(N_LANE_TILES):
        xp_ref[t] = jnp.concatenate(
            [u_ref[t, pl.ds(j, rows, stride=PACK), :] for j in range(PACK)],
            axis=1).astype(_BF16)
        z = _dot(xp_ref[t], w1_ref[t])
        zr_ref[:, t * STATE_PER_TILE:(t + 1) * STATE_PER_TILE] = z[:, :STATE_PER_TILE]
        zi_ref[:, t * STATE_PER_TILE:(t + 1) * STATE_PER_TILE] = z[:, STATE_PER_TILE:]

    a_r = a8_ref[0]
    a_i = a8_ref[1]
    s_r = carry_ref[0]
    s_i = carry_ref[1]
    seg = rows // N_LANE_TILES
    for t in range(N_LANE_TILES):
        yi_ref[t] = _dot(xp_ref[t], tm_ref[t])
        for r in range(t * seg, (t + 1) * seg):
            sr_ref[r:r + 1, :] = s_r
            si_ref[r:r + 1, :] = s_i
            z_r = zr_ref[r:r + 1, :]
            z_i = zi_ref[r:r + 1, :]
            s_r, s_i = a_r * s_r - a_i * s_i + z_r, a_r * s_i + a_i * s_r + z_i
    carry_ref[0] = s_r
    carry_ref[1] = s_i

    for t in range(N_LANE_TILES):
        sl = slice(t * STATE_PER_TILE, (t + 1) * STATE_PER_TILE)
        sprev = jnp.concatenate([sr_ref[:, sl], si_ref[:, sl]], axis=1).astype(_BF16)
        y = yi_ref[t] + _dot(sprev, w3_ref[t])
        for i in range(PACK):
            y_ref[t, pl.ds(i, rows, stride=PACK), :] = y[:, i * LANES:(i + 1) * LANES]
        cols = slice(t * LANES, (t + 1) * LANES)
        o_ref[:, cols] = jax.nn.gelu(y_ref[t] + d_ref[:, cols] * u_ref[t]).astype(o_ref.dtype)


def _s5_call(x2d, w_in_b, pk, bt, pwc, ct, a8, d_skip, seq_len):
    n = x2d.shape[0]
    tokens = S5_ROWS * PACK
    steps_per_seq = seq_len // tokens
    u_block = (GATE_COLS + 3 * CONV_WIDTH) // SSM_WIDTH
    table = pltpu.VMEM((N_LANE_TILES, PACKED_TILE, PACKED_TILE), _BF16)
    slab = pltpu.VMEM((N_LANE_TILES, tokens, LANES), _F32)
    state = pltpu.VMEM((S5_ROWS, N_STATE), _F32)
    return pl.pallas_call(
        functools.partial(_s5_kernel, steps_per_seq),
        out_shape=jax.ShapeDtypeStruct((n, SSM_WIDTH), _BF16),
        grid=(n // tokens,),
        in_specs=[pl.BlockSpec((tokens, D_MODEL), lambda i: (i, 0)),
                  pl.BlockSpec((D_MODEL, SSM_WIDTH), lambda i: (0, u_block),
                               pipeline_mode=pl.Buffered(1))]
                 + [_const_spec(c.shape) for c in (pk, bt, pwc, ct, a8, d_skip)],
        out_specs=pl.BlockSpec((tokens, SSM_WIDTH), lambda i: (i, 0)),
        scratch_shapes=[table, table, table, slab,
                        pltpu.VMEM((N_LANE_TILES, S5_ROWS, PACKED_TILE), _BF16),
                        pltpu.VMEM((N_LANE_TILES, S5_ROWS, PACKED_TILE), _F32),
                        slab, state, state, state, state,
                        pltpu.VMEM((2, 1, N_STATE), _F32)],
        compiler_params=pltpu.CompilerParams(dimension_semantics=("arbitrary",),
                                             vmem_limit_bytes=VMEM_LIMIT),
        name="s5",
    )(x2d, w_in_b, pk, bt, pwc, ct, a8, d_skip)


def _mixer_kernel(steps_per_seq, x_ref, ys_ref, kv_ref, wg_ref, bg_ref, wc_ref, cw_ref,
                  wco_ref, wglu_ref, wq_ref, wxo_ref, wo_ref, g_ref, b_ref,
                  o_ref, zbuf_ref):
    tm_ = x_ref.shape[0]

    @pl.when(pl.program_id(0) % steps_per_seq == 0)
    def _():
        zbuf_ref[0:SUBLANES, :] = jnp.zeros((SUBLANES, CONV_WIDTH), _F32)

    def project(r):
        rows = slice(r * MIX_SUB, (r + 1) * MIX_SUB)
        xb = x_ref[rows, :].astype(_BF16)
        q = _dot(xb, wq_ref[...]).astype(_BF16)
        conv_in = [_dot(xb, wc_ref[:, k * CONV_WIDTH:(k + 1) * CONV_WIDTH]) for k in range(3)]
        scores = []
        for h in range(XATTN_HEADS):
            hs = slice(h * XATTN_HEAD_DIM, (h + 1) * XATTN_HEAD_DIM)
            scores.append(lax.dot_general(q[:, hs], kv_ref[:, hs], (((1,), (1,)), ((), ())),
                                          preferred_element_type=_F32))
        gates = [_dot(xb, wg_ref[:, k * D_MODEL:(k + 1) * D_MODEL]) for k in range(N_BRANCH)]
        return conv_in, scores, gates

    def mix(r, conv_in, scores, gates):
        r0 = r * MIX_SUB
        rows = slice(r0, r0 + MIX_SUB)
        cb, cc, ch = conv_in

        def gate(k):
            return jax.nn.sigmoid(gates[k] + bg_ref[:, k * D_MODEL:(k + 1) * D_MODEL])

        z = cc * ch
        zbuf_ref[SUBLANES + r0:SUBLANES + r0 + MIX_SUB, :] = z
        conv = (cw_ref[0:1, :] * zbuf_ref[SUBLANES - 2 + r0:SUBLANES - 2 + r0 + MIX_SUB, :]
                + cw_ref[1:2, :] * zbuf_ref[SUBLANES - 1 + r0:SUBLANES - 1 + r0 + MIX_SUB, :]
                + cw_ref[2:3, :] * z)
        y_a = _dot((cb * conv).astype(_BF16), wco_ref[...])

        heads = []
        for h in range(XATTN_HEADS):
            vs = slice(XATTN_WIDTH + h * XATTN_HEAD_DIM, XATTN_WIDTH + (h + 1) * XATTN_HEAD_DIM)
            s = scores[h] * (XATTN_HEAD_DIM ** -0.5)
            p = jnp.exp(s - jnp.max(s, axis=-1, keepdims=True))
            den = jnp.sum(p, axis=-1, keepdims=True)
            heads.append((_dot(p.astype(_BF16), kv_ref[:, vs]) / den).astype(_BF16))
        y_c = _dot(jnp.concatenate(heads, axis=1), wxo_ref[...])
        merged = gate(0) * y_a + gate(2) * y_c

        glu = _dot(ys_ref[rows, :], wglu_ref[...])
        merged += gate(1) * (glu[:, :D_MODEL] * jax.nn.sigmoid(glu[:, D_MODEL:]))

        v = ALPHA * x_ref[rows, :] + _dot(merged.astype(_BF16), wo_ref[...])
        o_ref[rows, :] = _layer_norm(v, g_ref[...], b_ref[...])

    n_chains = tm_ // MIX_SUB
    pending = project(0)
    for r in range(n_chains):
        current = pending
        if r + 1 < n_chains:
            pending = project(r + 1)
        mix(r, *current)

    zbuf_ref[0:SUBLANES, :] = zbuf_ref[tm_:tm_ + SUBLANES, :]


def _col_block_spec(width, start):
    assert start % width == 0
    return pl.BlockSpec((D_MODEL, width), lambda i: (0, start // width),
                        pipeline_mode=pl.Buffered(1))


def _mixer_call(x2d, ys, kv, w_in_b, b_g, conv_w, w_co, w_glu, w_xo, w_o, ln_g, ln_b, seq_len):
    n = x2d.shape[0]
    steps_per_seq = seq_len // MIX_TOKENS
    conv_start = GATE_COLS
    q_start = GATE_COLS + 3 * CONV_WIDTH + SSM_WIDTH
    operands = [w_in_b, b_g, w_in_b, conv_w, w_co, w_glu, w_in_b, w_xo, w_o, ln_g, ln_b]
    specs = [_col_block_spec(GATE_COLS, 0), _const_spec(b_g.shape),
             _col_block_spec(3 * CONV_WIDTH, conv_start)]
    specs += [_const_spec(c.shape) for c in (conv_w, w_co, w_glu)]
    specs += [_col_block_spec(XATTN_WIDTH, q_start)]
    specs += [_const_spec(c.shape) for c in (w_xo, w_o, ln_g, ln_b)]
    return pl.pallas_call(
        functools.partial(_mixer_kernel, steps_per_seq),
        out_shape=jax.ShapeDtypeStruct((n, D_MODEL), _F32),
        grid=(n // MIX_TOKENS,),
        in_specs=[pl.BlockSpec((MIX_TOKENS, D_MODEL), lambda i: (i, 0)),
                  pl.BlockSpec((MIX_TOKENS, SSM_WIDTH), lambda i: (i, 0)),
                  pl.BlockSpec((MEM_LEN, 2 * XATTN_WIDTH), lambda i: (i // steps_per_seq, 0))]
                 + specs,
        out_specs=pl.BlockSpec((MIX_TOKENS, D_MODEL), lambda i: (i, 0)),
        scratch_shapes=[pltpu.VMEM((MIX_TOKENS + 2 * SUBLANES, CONV_WIDTH), _F32)],
        compiler_params=pltpu.CompilerParams(dimension_semantics=("arbitrary",),
                                             vmem_limit_bytes=VMEM_LIMIT),
        name="mixers",
    )(x2d, ys, kv, *operands)


def _mlp_kernel(x_ref, wu_ref, bu_ref, wd_ref, bd_ref, g_ref, b_ref, o_ref):
    for r in range(MLP_TOKENS // MLP_SUB):
        rows = slice(r * MLP_SUB, (r + 1) * MLP_SUB)
        x = x_ref[rows, :]
        xb = x.astype(_BF16)
        acc = ALPHA * x + bd_ref[...]
        for c in range(D_FF // FF_CHUNK):
            cols = slice(c * FF_CHUNK, (c + 1) * FF_CHUNK)
            h = jnp.maximum(_dot(xb, wu_ref[:, cols]) + bu_ref[:, cols], 0.0)
            acc += _dot((h * h).astype(_BF16), wd_ref[cols, :])
        o_ref[rows, :] = _layer_norm(acc, g_ref[...], b_ref[...])


def _mlp_call(x1, w_up, b_up, w_down, b_down, ln_g, ln_b):
    n = x1.shape[0]
    consts = [w_up, b_up, w_down, b_down, ln_g, ln_b]
    return pl.pallas_call(
        _mlp_kernel,
        out_shape=jax.ShapeDtypeStruct((n, D_MODEL), _F32),
        grid=(n // MLP_TOKENS,),
        in_specs=[pl.BlockSpec((MLP_TOKENS, D_MODEL), lambda i: (i, 0))]
                 + [_const_spec(c.shape) for c in consts],
        out_specs=pl.BlockSpec((MLP_TOKENS, D_MODEL), lambda i: (i, 0)),
        compiler_params=pltpu.CompilerParams(dimension_semantics=("arbitrary",),
                                             vmem_limit_bytes=VMEM_LIMIT),
        name="mlp",
    )(x1, *consts)


def kernel(x, mem, w_in, b_gate, conv_w, w_conv_out, ssm_lam_re, ssm_lam_im, ssm_log_dt,
           ssm_b_re, ssm_b_im, ssm_c_re, ssm_c_im, ssm_d, w_glu, w_kv, w_xattn_out, w_out,
           ln1_g, ln1_b, w_up, b_up, w_down, b_down, ln2_g, ln2_b):
    bsz, seq, d = x.shape
    assert d == D_MODEL and seq % (PACK * S5_ROWS) == 0 and seq % MIX_TOKENS == 0
    assert w_in.shape[0] == DEPTH == 1
    n = bsz * seq
    l = 0
    row = lambda a: a.reshape(1, -1).astype(_F32)

    w_in_b = w_in[l].astype(_BF16)
    tables = _s5_prep(ssm_lam_re[l], ssm_lam_im[l], ssm_log_dt[l], ssm_b_re[l],
                      ssm_b_im[l], ssm_c_re[l], ssm_c_im[l])

    x2d = x.reshape(n, d)
    kv = _kv_call(mem.reshape(bsz * MEM_LEN, d), w_kv[l].astype(_BF16))
    ys = _s5_call(x2d, w_in_b, *tables, row(ssm_d[l]), seq)
    x1 = _mixer_call(x2d, ys, kv, w_in_b, row(b_gate[l]), conv_w[l].astype(_F32),
                     w_conv_out[l].astype(_BF16), w_glu[l].astype(_BF16),
                     w_xattn_out[l].astype(_BF16), w_out[l].astype(_BF16),
                     row(ln1_g[l]), row(ln1_b[l]), seq)
    out = _mlp_call(x1, w_up[l].astype(_BF16), row(b_up[l]), w_down[l].astype(_BF16),
                    row(b_down[l]), row(ln2_g[l]), row(ln2_b[l]))
    return out.reshape(bsz, seq, d)
```

```python
import functools
import math

import jax
import jax.numpy as jnp
from jax import lax
from jax.experimental import pallas as pl
from jax.experimental.pallas import tpu as pltpu

D_MODEL = 1024
MEM_LEN = 256
N_BRANCH = 3
CONV_WIDTH = 512
CONV_TAPS = 3
SSM_WIDTH = 512
SSM_GROUP = 16
SSM_GROUPS = 32
SSM_STATE = 64
XATTN_HEADS = 4
XATTN_HEAD_DIM = 128
XATTN_WIDTH = 512
D_FF = 4096
GATE_COLS = N_BRANCH * D_MODEL
DEPTH = 1
ALPHA = (2.0 * DEPTH) ** 0.25
LN_EPS = 1e-5

LANES = 128
SUBLANES = 8
PACK = SUBLANES
GROUPS_PER_TILE = LANES // SSM_GROUP
N_LANE_TILES = SSM_WIDTH // LANES
STATE_PER_TILE = GROUPS_PER_TILE * SSM_STATE
N_STATE = SSM_GROUPS * SSM_STATE
PACKED_TILE = PACK * LANES

S5_ROWS = 128
MIX_TOKENS = 1024
MIX_SUB = 256
MLP_TOKENS = 1024
MLP_SUB = 256
FF_CHUNK = 1024
VMEM_LIMIT = 60 * 1024 * 1024

_BF16 = jnp.bfloat16
_F32 = jnp.float32


def _const_spec(shape):
    zeros = (0,) * len(shape)
    return pl.BlockSpec(shape, lambda *_: zeros, pipeline_mode=pl.Buffered(1))


def _dot(a, b):
    return jnp.dot(a, b, preferred_element_type=_F32)


def _layer_norm(v, g, b):
    mu = jnp.mean(v, axis=-1, keepdims=True)
    c = v - mu
    var = jnp.mean(c * c, axis=-1, keepdims=True)
    return c * lax.rsqrt(var + LN_EPS) * g + b


def _complex_powers(ar, ai, n):
    out = [(jnp.ones_like(ar), jnp.zeros_like(ai))]
    for _ in range(n):
        pr, pi = out[-1]
        out.append((pr * ar - pi * ai, pr * ai + pi * ar))
    return out


def _build_tables(rowp_ref, colp_ref, bt_ref, ct_ref, w1_ref, tm_ref, w3_ref, a8_ref):
    hp = lax.Precision.HIGHEST
    row_pow = _complex_powers(rowp_ref[0:1, :], rowp_ref[1:2, :], PACK)
    col_pow = _complex_powers(colp_ref[:, 0:1], colp_ref[:, 1:2], PACK)
    zr, zi = rowp_ref[2:3, :], rowp_ref[3:4, :]
    a8_ref[0], a8_ref[1] = row_pow[PACK]
    rows1 = lax.broadcasted_iota(jnp.int32, (LANES, STATE_PER_TILE), 0) // SSM_GROUP
    cols1 = lax.broadcasted_iota(jnp.int32, (LANES, STATE_PER_TILE), 1) // SSM_STATE
    same1 = rows1 == cols1
    rows3 = lax.broadcasted_iota(jnp.int32, (STATE_PER_TILE, LANES), 0) // SSM_STATE
    cols3 = lax.broadcasted_iota(jnp.int32, (STATE_PER_TILE, LANES), 1) // SSM_GROUP
    same3 = rows3 == cols3

    lane_reps = STATE_PER_TILE // LANES
    btr = jnp.concatenate([bt_ref[0]] * lane_reps, axis=1)
    bti = jnp.concatenate([bt_ref[1]] * lane_reps, axis=1)
    bbar = None
    for j in range(PACK):
        qr, qi = row_pow[PACK - 1 - j]
        pr, pi = qr * zr - qi * zi, qr * zi + qi * zr
        sre = jnp.where(same1, pr * btr - pi * bti, 0.0)
        sim = jnp.where(same1, pr * bti + pi * btr, 0.0)
        w1_ref[j * LANES:(j + 1) * LANES, :STATE_PER_TILE] = sre.astype(w1_ref.dtype)
        w1_ref[j * LANES:(j + 1) * LANES, STATE_PER_TILE:] = sim.astype(w1_ref.dtype)
        if j == PACK - 1:
            bbar = jnp.concatenate([sre, sim], axis=1)

    ctr, cti = ct_ref[0], ct_ref[1]
    tker = []
    for q in range(PACK + 1):
        pr, pi = col_pow[q]
        ere = jnp.where(same3, ctr * pr - cti * pi, 0.0)
        eim = jnp.where(same3, ctr * pi + cti * pr, 0.0)
        slab = jnp.concatenate([ere, -eim], axis=0)
        if q >= 1:
            w3_ref[:, (q - 1) * LANES:q * LANES] = slab.astype(w3_ref.dtype)
        if q < PACK:
            tker.append(jnp.dot(bbar, slab, precision=hp,
                                preferred_element_type=_F32).astype(tm_ref.dtype))

    zero = jnp.zeros((LANES, LANES), tm_ref.dtype)
    for j in range(PACK):
        for i in range(PACK):
            tm_ref[j * LANES:(j + 1) * LANES, i * LANES:(i + 1) * LANES] = (
                tker[i - j] if i >= j else zero)


def _s5_prep(lam_re, lam_im, log_dt, b_re, b_im, c_re, c_im):
    lr, li = lam_re.astype(_F32), lam_im.astype(_F32)
    dt = jnp.exp(log_dt.astype(_F32))[:, None]
    mag = jnp.exp(lr * dt)
    ar = mag * jnp.cos(li * dt)
    ai = mag * jnp.sin(li * dt)
    den = lr * lr + li * li
    nr, ni = ar - 1.0, ai
    kr = (nr * lr + ni * li) / den
    ki = (ni * lr - nr * li) / den
    t, g = N_LANE_TILES, GROUPS_PER_TILE

    rowp = jnp.stack([ar, ai, kr, ki]).reshape(4, t, STATE_PER_TILE).transpose(1, 0, 2)
    colp = rowp.transpose(0, 2, 1)
    b = jnp.stack([b_re, b_im]).astype(_F32).reshape(2, t, g, SSM_STATE, SSM_GROUP)
    reps = LANES // SSM_STATE
    bt = jnp.broadcast_to(b.transpose(1, 0, 2, 4, 3)[:, :, :, :, None, :],
                          (t, 2, g, SSM_GROUP, reps, SSM_STATE)).reshape(t, 2, LANES, LANES)
    c = jnp.stack([c_re, c_im]).astype(_F32).reshape(2, t, g, SSM_GROUP, SSM_STATE)
    ct = jnp.broadcast_to(c.transpose(1, 0, 2, 4, 3)[:, :, :, :, None, :],
                          (t, 2, g, SSM_STATE, g, SSM_GROUP)).reshape(t, 2, STATE_PER_TILE, LANES)
    return rowp, colp, bt, ct


def _kv_kernel(mem_ref, w_ref, o_ref):
    o_ref[...] = _dot(mem_ref[...].astype(_BF16), w_ref[...]).astype(o_ref.dtype)


def _kv_call(mem2d, w_kv):
    rows = mem2d.shape[0]
    return pl.pallas_call(
        _kv_kernel,
        out_shape=jax.ShapeDtypeStruct((rows, 2 * XATTN_WIDTH), _BF16),
        grid=(1,),
        in_specs=[pl.BlockSpec(mem2d.shape, lambda i: (0, 0)),
                  pl.BlockSpec(w_kv.shape, lambda i: (0, 0))],
        out_specs=pl.BlockSpec((rows, 2 * XATTN_WIDTH), lambda i: (0, 0)),
        compiler_params=pltpu.CompilerParams(vmem_limit_bytes=VMEM_LIMIT),
        name="kv",
    )(mem2d, w_kv)


def _s5_kernel(steps_per_seq, x_ref, wu_ref, rowp_ref, colp_ref, bt_ref, ct_ref, d_ref, cast_ref,
               o_ref, cast_out_ref, w1_ref, tm_ref, w3_ref, a8t_ref, a8_ref, u_ref, xp_ref, yi_ref, y_ref,
               zr_ref, zi_ref, sr_ref, si_ref, carry_ref):
    rows = x_ref.shape[0] // PACK

    cast_out_ref[...] = cast_ref[...].astype(cast_out_ref.dtype)

    @pl.when(pl.program_id(0) == 0)
    def _():
        def build(t, c):
            _build_tables(rowp_ref.at[t], colp_ref.at[t], bt_ref.at[t], ct_ref.at[t],
                          w1_ref.at[t], tm_ref.at[t], w3_ref.at[t], a8t_ref.at[t])
            return c
        lax.fori_loop(0, N_LANE_TILES, build, 0)
        for t in range(N_LANE_TILES):
            a8_ref[:, :, t * STATE_PER_TILE:(t + 1) * STATE_PER_TILE] = a8t_ref[t]

    @pl.when(pl.program_id(0) % steps_per_seq == 0)
    def _():
        carry_ref[...] = jnp.zeros_like(carry_ref)

    u = _dot(x_ref[...].astype(_BF16), wu_ref[...])
    for t in range(N_LANE_TILES):
        u_ref[t] = u[:, t * LANES:(t + 1) * LANES]

    for t in range(N_LANE_TILES):
        xp_ref[t] = jnp.concatenate(
            [u_ref[t, pl.ds(j, rows, stride=PACK), :] for j in range(PACK)],
            axis=1).astype(_BF16)
        z = _dot(xp_ref[t], w1_ref[t])
        zr_ref[:, t * STATE_PER_TILE:(t + 1) * STATE_PER_TILE] = z[:, :STATE_PER_TILE]
        zi_ref[:, t * STATE_PER_TILE:(t + 1) * STATE_PER_TILE] = z[:, STATE_PER_TILE:]

    a_r = a8_ref[0]
    a_i = a8_ref[1]
    s_r = carry_ref[0]
    s_i = carry_ref[1]
    seg = rows // N_LANE_TILES
    for t in range(N_LANE_TILES):
        yi_ref[t] = _dot(xp_ref[t], tm_ref[t])
        for r in range(t * seg, (t + 1) * seg):
            sr_ref[r:r + 1, :] = s_r
            si_ref[r:r + 1, :] = s_i
            z_r = zr_ref[r:r + 1, :]
            z_i = zi_ref[r:r + 1, :]
            s_r, s_i = a_r * s_r - a_i * s_i + z_r, a_r * s_i + a_i * s_r + z_i
    carry_ref[0] = s_r
    carry_ref[1] = s_i

    for t in range(N_LANE_TILES):
        sl = slice(t * STATE_PER_TILE, (t + 1) * STATE_PER_TILE)
        sprev = jnp.concatenate([sr_ref[:, sl], si_ref[:, sl]], axis=1).astype(_BF16)
        y = yi_ref[t] + _dot(sprev, w3_ref[t])
        for i in range(PACK):
            y_ref[t, pl.ds(i, rows, stride=PACK), :] = y[:, i * LANES:(i + 1) * LANES]
        cols = slice(t * LANES, (t + 1) * LANES)
        o_ref[:, cols] = jax.nn.gelu(y_ref[t] + d_ref[:, cols] * u_ref[t]).astype(o_ref.dtype)


def _s5_call(x2d, w_in_b, tables, d_skip, layer, w_cast, seq_len):
    rowp, colp, bt, ct = tables
    n = x2d.shape[0]
    tokens = S5_ROWS * PACK
    steps = n // tokens
    steps_per_seq = seq_len // tokens
    u_block = (GATE_COLS + 3 * CONV_WIDTH) // SSM_WIDTH
    cast_rows, cast_cols = w_cast.shape[1:]
    cast_block = cast_cols // steps
    assert cast_block * steps == cast_cols and cast_block % LANES == 0
    table = pltpu.VMEM((N_LANE_TILES, PACKED_TILE, PACKED_TILE), _BF16)
    slab = pltpu.VMEM((N_LANE_TILES, tokens, LANES), _F32)
    state = pltpu.VMEM((S5_ROWS, N_STATE), _F32)
    return pl.pallas_call(
        functools.partial(_s5_kernel, steps_per_seq),
        out_shape=(jax.ShapeDtypeStruct((n, SSM_WIDTH), _BF16),
                   jax.ShapeDtypeStruct((cast_rows, cast_cols), _BF16)),
        grid=(steps,),
        in_specs=[pl.BlockSpec((tokens, D_MODEL), lambda i: (i, 0)),
                  pl.BlockSpec((D_MODEL, SSM_WIDTH), lambda i: (0, u_block),
                               pipeline_mode=pl.Buffered(1))]
                 + [_const_spec(c.shape) for c in (rowp, colp, bt, ct)]
                 + [_layer_spec(d_skip, layer),
                    pl.BlockSpec((None, cast_rows, cast_block), lambda i: (layer, 0, i))],
        out_specs=(pl.BlockSpec((tokens, SSM_WIDTH), lambda i: (i, 0)),
                   pl.BlockSpec((cast_rows, cast_block), lambda i: (0, i))),
        scratch_shapes=[table, table, table,
                        pltpu.VMEM((N_LANE_TILES, 2, 1, STATE_PER_TILE), _F32),
                        pltpu.VMEM((2, 1, N_STATE), _F32), slab,
                        pltpu.VMEM((N_LANE_TILES, S5_ROWS, PACKED_TILE), _BF16),
                        pltpu.VMEM((N_LANE_TILES, S5_ROWS, PACKED_TILE), _F32),
                        slab, state, state, state, state,
                        pltpu.VMEM((2, 1, N_STATE), _F32)],
        compiler_params=pltpu.CompilerParams(dimension_semantics=("arbitrary",),
                                             vmem_limit_bytes=VMEM_LIMIT),
        name="s5",
    )(x2d, w_in_b, rowp, colp, bt, ct, d_skip, w_cast)


def _mixer_kernel(steps_per_seq, x_ref, ys_ref, kv_ref, wg_ref, bg_ref, wc_ref, cw_ref,
                  wco_ref, wglu_ref, wq_ref, wxo_ref, wo_ref, g_ref, b_ref, cast_ref,
                  o_ref, cast_out_ref, zbuf_ref):
    tm_ = x_ref.shape[0]

    cast_out_ref[...] = cast_ref[...].astype(cast_out_ref.dtype)

    @pl.when(pl.program_id(0) % steps_per_seq == 0)
    def _():
        zbuf_ref[0:SUBLANES, :] = jnp.zeros((SUBLANES, CONV_WIDTH), _F32)

    def project(r):
        rows = slice(r * MIX_SUB, (r + 1) * MIX_SUB)
        xb = x_ref[rows, :].astype(_BF16)
        q = _dot(xb, wq_ref[...]).astype(_BF16)
        conv_in = [_dot(xb, wc_ref[:, k * CONV_WIDTH:(k + 1) * CONV_WIDTH]) for k in range(3)]
        scores = []
        for h in range(XATTN_HEADS):
            hs = slice(h * XATTN_HEAD_DIM, (h + 1) * XATTN_HEAD_DIM)
            scores.append(lax.dot_general(q[:, hs], kv_ref[:, hs], (((1,), (1,)), ((), ())),
                                          preferred_element_type=_F32))
        gates = [_dot(xb, wg_ref[:, k * D_MODEL:(k + 1) * D_MODEL]) for k in range(N_BRANCH)]
        return conv_in, scores, gates

    def mix(r, conv_in, scores, gates):
        r0 = r * MIX_SUB
        rows = slice(r0, r0 + MIX_SUB)
        cb, cc, ch = conv_in

        def gate(k):
            return jax.nn.sigmoid(gates[k] + bg_ref[:, k * D_MODEL:(k + 1) * D_MODEL])

        glu = _dot(ys_ref[rows, :], wglu_ref[...])

        z = cc * ch
        zbuf_ref[SUBLANES + r0:SUBLANES + r0 + MIX_SUB, :] = z
        conv = (cw_ref[0:1, :] * zbuf_ref[SUBLANES - 2 + r0:SUBLANES - 2 + r0 + MIX_SUB, :]
                + cw_ref[1:2, :] * zbuf_ref[SUBLANES - 1 + r0:SUBLANES - 1 + r0 + MIX_SUB, :]
                + cw_ref[2:3, :] * z)
        y_a = _dot((cb * conv).astype(_BF16), wco_ref[...])

        heads = []
        for h in range(XATTN_HEADS):
            vs = slice(XATTN_WIDTH + h * XATTN_HEAD_DIM, XATTN_WIDTH + (h + 1) * XATTN_HEAD_DIM)
            s = scores[h] * (XATTN_HEAD_DIM ** -0.5)
            p = jnp.exp(s - jnp.max(s, axis=-1, keepdims=True))
            den = jnp.sum(p, axis=-1, keepdims=True)
            heads.append((_dot(p.astype(_BF16), kv_ref[:, vs]) / den).astype(_BF16))
        y_c = _dot(jnp.concatenate(heads, axis=1), wxo_ref[...])

        merged = gate(0) * y_a + gate(2) * y_c
        merged += gate(1) * (glu[:, :D_MODEL] * jax.nn.sigmoid(glu[:, D_MODEL:]))
        v = ALPHA * x_ref[rows, :] + _dot(merged.astype(_BF16), wo_ref[...])
        o_ref[rows, :] = _layer_norm(v, g_ref[...], b_ref[...])

    n_chains = tm_ // MIX_SUB
    pending = project(0)
    for r in range(n_chains):
        current = pending
        if r + 1 < n_chains:
            pending = project(r + 1)
        mix(r, *current)

    zbuf_ref[0:SUBLANES, :] = zbuf_ref[tm_:tm_ + SUBLANES, :]


def _col_block_spec(width, start):
    assert start % width == 0
    return pl.BlockSpec((D_MODEL, width), lambda i: (0, start // width),
                        pipeline_mode=pl.Buffered(1))


def _layer_spec(param, layer):
    rest = param.shape[1:]
    if len(rest) == 1:
        return pl.BlockSpec((1,) + rest, lambda *_: (layer, 0), pipeline_mode=pl.Buffered(1))
    zeros = (0,) * len(rest)
    return pl.BlockSpec((None,) + rest, lambda *_: (layer,) + zeros,
                        pipeline_mode=pl.Buffered(1))


def _mixer_call(x2d, ys, kv, w_in_b, b_g, conv_w, w_co, w_glu, w_xo, w_o, ln_g, ln_b,
                layer, w_cast, seq_len):
    n = x2d.shape[0]
    steps = n // MIX_TOKENS
    steps_per_seq = seq_len // MIX_TOKENS
    conv_start = GATE_COLS
    q_start = GATE_COLS + 3 * CONV_WIDTH + SSM_WIDTH
    cast_rows, cast_cols = w_cast.shape[1:]
    cast_block = cast_rows // steps
    assert cast_block * steps == cast_rows and cast_block % (2 * SUBLANES) == 0
    operands = [w_in_b, b_g, w_in_b, conv_w, w_co, w_glu, w_in_b, w_xo, w_o, ln_g, ln_b, w_cast]
    specs = [_col_block_spec(GATE_COLS, 0), _layer_spec(b_g, layer),
             _col_block_spec(3 * CONV_WIDTH, conv_start), _layer_spec(conv_w, layer)]
    specs += [_const_spec(c.shape) for c in (w_co, w_glu)]
    specs += [_col_block_spec(XATTN_WIDTH, q_start)]
    specs += [_const_spec(c.shape) for c in (w_xo, w_o)]
    specs += [_layer_spec(ln_g, layer), _layer_spec(ln_b, layer),
              pl.BlockSpec((None, cast_block, cast_cols), lambda i: (layer, i, 0))]
    return pl.pallas_call(
        functools.partial(_mixer_kernel, steps_per_seq),
        out_shape=(jax.ShapeDtypeStruct((n, D_MODEL), _F32),
                   jax.ShapeDtypeStruct((cast_rows, cast_cols), _BF16)),
        grid=(steps,),
        in_specs=[pl.BlockSpec((MIX_TOKENS, D_MODEL), lambda i: (i, 0)),
                  pl.BlockSpec((MIX_TOKENS, SSM_WIDTH), lambda i: (i, 0)),
                  pl.BlockSpec((MEM_LEN, 2 * XATTN_WIDTH), lambda i: (i // steps_per_seq, 0))]
                 + specs,
        out_specs=(pl.BlockSpec((MIX_TOKENS, D_MODEL), lambda i: (i, 0)),
                   pl.BlockSpec((cast_block, cast_cols), lambda i: (i, 0))),
        scratch_shapes=[pltpu.VMEM((MIX_TOKENS + 2 * SUBLANES, CONV_WIDTH), _F32)],
        compiler_params=pltpu.CompilerParams(dimension_semantics=("arbitrary",),
                                             vmem_limit_bytes=VMEM_LIMIT),
        name="mixers",
    )(x2d, ys, kv, *operands)


def _mlp_kernel(x_ref, wu_ref, bu_ref, wd_ref, bd_ref, g_ref, b_ref, o_ref):
    n_chunks = D_FF // FF_CHUNK
    units = [(r, c) for r in range(MLP_TOKENS // MLP_SUB) for c in range(n_chunks)]
    xb, acc = {}, {}

    def up(r, c):
        if c == 0:
            xb[r] = x_ref[r * MLP_SUB:(r + 1) * MLP_SUB, :].astype(_BF16)
        return _dot(xb[r], wu_ref[:, c * FF_CHUNK:(c + 1) * FF_CHUNK])

    def down(r, c, pre):
        cols = slice(c * FF_CHUNK, (c + 1) * FF_CHUNK)
        h = jnp.maximum(pre + bu_ref[:, cols], 0.0)
        if c == 0:
            acc[r] = ALPHA * x_ref[r * MLP_SUB:(r + 1) * MLP_SUB, :] + bd_ref[...]
        acc[r] += _dot((h * h).astype(_BF16), wd_ref[cols, :])

    pre = up(*units[0])
    for k, (r, c) in enumerate(units):
        nxt = up(*units[k + 1]) if k + 1 < len(units) else None
        down(r, c, pre)
        if c == n_chunks - 1:
            o_ref[r * MLP_SUB:(r + 1) * MLP_SUB, :] = _layer_norm(acc.pop(r), g_ref[...],
                                                                  b_ref[...])
        pre = nxt


def _mlp_call(x1, w_up, b_up, w_down, b_down, ln_g, ln_b, layer):
    n = x1.shape[0]
    consts = [w_up, b_up, w_down, b_down, ln_g, ln_b]
    return pl.pallas_call(
        _mlp_kernel,
        out_shape=jax.ShapeDtypeStruct((n, D_MODEL), _F32),
        grid=(n // MLP_TOKENS,),
        in_specs=[pl.BlockSpec((MLP_TOKENS, D_MODEL), lambda i: (i, 0)),
                  _const_spec(w_up.shape), _layer_spec(b_up, layer),
                  _const_spec(w_down.shape), _layer_spec(b_down, layer),
                  _layer_spec(ln_g, layer), _layer_spec(ln_b, layer)],
        out_specs=pl.BlockSpec((MLP_TOKENS, D_MODEL), lambda i: (i, 0)),
        compiler_params=pltpu.CompilerParams(dimension_semantics=("arbitrary",),
                                             vmem_limit_bytes=VMEM_LIMIT),
        name="mlp",
    )(x1, *consts)


def kernel(x, mem, w_in, b_gate, conv_w, w_conv_out, ssm_lam_re, ssm_lam_im, ssm_log_dt,
           ssm_b_re, ssm_b_im, ssm_c_re, ssm_c_im, ssm_d, w_glu, w_kv, w_xattn_out, w_out,
           ln1_g, ln1_b, w_up, b_up, w_down, b_down, ln2_g, ln2_b):
    bsz, seq, d = x.shape
    assert d == D_MODEL and seq % (PACK * S5_ROWS) == 0 and seq % MIX_TOKENS == 0
    assert w_in.shape[0] == DEPTH == 1
    n = bsz * seq
    l = 0

    w_in_b = w_in[l].astype(_BF16)
    tables = _s5_prep(ssm_lam_re[l], ssm_lam_im[l], ssm_log_dt[l], ssm_b_re[l],
                      ssm_b_im[l], ssm_c_re[l], ssm_c_im[l])

    x2d = x.reshape(n, d)
    kv = _kv_call(mem.reshape(bsz * MEM_LEN, d), w_kv[l].astype(_BF16))
    ys, w_up_b = _s5_call(x2d, w_in_b, tables, ssm_d, l, w_up, seq)
    x1, w_down_b = _mixer_call(x2d, ys, kv, w_in_b, b_gate, conv_w,
                               w_conv_out[l].astype(_BF16), w_glu[l].astype(_BF16),
                               w_xattn_out[l].astype(_BF16), w_out[l].astype(_BF16),
                               ln1_g, ln1_b, l, w_down, seq)
    out = _mlp_call(x1, w_up_b, b_up, w_down_b, b_down, ln2_g, ln2_b, l)
    return out.reshape(bsz, seq, d)
```

```python
import functools
import math

import jax
import jax.numpy as jnp
from jax import lax
from jax.experimental import pallas as pl
from jax.experimental.pallas import tpu as pltpu

D_MODEL = 1024
MEM_LEN = 256
N_BRANCH = 3
CONV_WIDTH = 512
CONV_TAPS = 3
SSM_WIDTH = 512
SSM_GROUP = 16
SSM_GROUPS = 32
SSM_STATE = 64
XATTN_HEADS = 4
XATTN_HEAD_DIM = 128
XATTN_WIDTH = 512
D_FF = 4096
GATE_COLS = N_BRANCH * D_MODEL
DEPTH = 1
ALPHA = (2.0 * DEPTH) ** 0.25
LN_EPS = 1e-5

LANES = 128
SUBLANES = 8
PACK = SUBLANES
GROUPS_PER_TILE = LANES // SSM_GROUP
N_LANE_TILES = SSM_WIDTH // LANES
STATE_PER_TILE = GROUPS_PER_TILE * SSM_STATE
N_STATE = SSM_GROUPS * SSM_STATE
PACKED_TILE = PACK * LANES

S5_ROWS = 128
TM_PANEL = 256
MIX_TOKENS = 1024
MIX_SUB = 256
MLP_TOKENS = 2048
MLP_SUB = 256
FF_CHUNK = 1024
VMEM_LIMIT = 60 * 1024 * 1024

_BF16 = jnp.bfloat16
_F32 = jnp.float32


def _const_spec(shape):
    zeros = (0,) * len(shape)
    return pl.BlockSpec(shape, lambda *_: zeros, pipeline_mode=pl.Buffered(1))


def _dot(a, b):
    return jnp.dot(a, b, preferred_element_type=_F32)


def _layer_norm(v, g, b):
    mu = jnp.mean(v, axis=-1, keepdims=True)
    c = v - mu
    var = jnp.mean(c * c, axis=-1, keepdims=True)
    return c * lax.rsqrt(var + LN_EPS) * g + b


def _complex_powers(ar, ai, n):
    out = [(jnp.ones_like(ar), jnp.zeros_like(ai))]
    for _ in range(n):
        pr, pi = out[-1]
        out.append((pr * ar - pi * ai, pr * ai + pi * ar))
    return out


def _build_tables(rowp_ref, colp_ref, bt_ref, ct_ref, w1_ref, tm_ref, w3_ref, a8_ref):
    hp = lax.Precision.HIGHEST
    row_pow = _complex_powers(rowp_ref[0:1, :], rowp_ref[1:2, :], PACK)
    col_pow = _complex_powers(colp_ref[:, 0:1], colp_ref[:, 1:2], PACK)
    zr, zi = rowp_ref[2:3, :], rowp_ref[3:4, :]
    a8_ref[0], a8_ref[1] = row_pow[PACK]
    rows1 = lax.broadcasted_iota(jnp.int32, (LANES, STATE_PER_TILE), 0) // SSM_GROUP
    cols1 = lax.broadcasted_iota(jnp.int32, (LANES, STATE_PER_TILE), 1) // SSM_STATE
    same1 = rows1 == cols1
    rows3 = lax.broadcasted_iota(jnp.int32, (STATE_PER_TILE, LANES), 0) // SSM_STATE
    cols3 = lax.broadcasted_iota(jnp.int32, (STATE_PER_TILE, LANES), 1) // SSM_GROUP
    same3 = rows3 == cols3

    lane_reps = STATE_PER_TILE // LANES
    btr = jnp.concatenate([bt_ref[0]] * lane_reps, axis=1)
    bti = jnp.concatenate([bt_ref[1]] * lane_reps, axis=1)
    bbar = None
    for j in range(PACK):
        qr, qi = row_pow[PACK - 1 - j]
        pr, pi = qr * zr - qi * zi, qr * zi + qi * zr
        sre = jnp.where(same1, pr * btr - pi * bti, 0.0)
        sim = jnp.where(same1, pr * bti + pi * btr, 0.0)
        w1_ref[j * LANES:(j + 1) * LANES, :STATE_PER_TILE] = sre.astype(w1_ref.dtype)
        w1_ref[j * LANES:(j + 1) * LANES, STATE_PER_TILE:] = sim.astype(w1_ref.dtype)
        if j == PACK - 1:
            bbar = jnp.concatenate([sre, sim], axis=1)

    ctr, cti = ct_ref[0], ct_ref[1]
    tker = []
    for q in range(PACK + 1):
        pr, pi = col_pow[q]
        ere = jnp.where(same3, ctr * pr - cti * pi, 0.0)
        eim = jnp.where(same3, ctr * pi + cti * pr, 0.0)
        slab = jnp.concatenate([ere, -eim], axis=0)
        if q >= 1:
            w3_ref[:, (q - 1) * LANES:q * LANES] = slab.astype(w3_ref.dtype)
        if q < PACK:
            tker.append(jnp.dot(bbar, slab, precision=hp,
                                preferred_element_type=_F32).astype(tm_ref.dtype))

    zero = jnp.zeros((LANES, LANES), tm_ref.dtype)
    for j in range(PACK):
        for i in range(PACK):
            tm_ref[j * LANES:(j + 1) * LANES, i * LANES:(i + 1) * LANES] = (
                tker[i - j] if i >= j else zero)


def _s5_prep(lam_re, lam_im, log_dt, b_re, b_im, c_re, c_im):
    lr, li = lam_re.astype(_F32), lam_im.astype(_F32)
    dt = jnp.exp(log_dt.astype(_F32))[:, None]
    mag = jnp.exp(lr * dt)
    ar = mag * jnp.cos(li * dt)
    ai = mag * jnp.sin(li * dt)
    den = lr * lr + li * li
    nr, ni = ar - 1.0, ai
    kr = (nr * lr + ni * li) / den
    ki = (ni * lr - nr * li) / den
    t, g = N_LANE_TILES, GROUPS_PER_TILE

    rowp = jnp.stack([ar, ai, kr, ki]).reshape(4, t, STATE_PER_TILE).transpose(1, 0, 2)
    colp = rowp.transpose(0, 2, 1)
    b = jnp.stack([b_re, b_im]).astype(_F32).reshape(2, t, g, SSM_STATE, SSM_GROUP)
    reps = LANES // SSM_STATE
    bt = jnp.broadcast_to(b.transpose(1, 0, 2, 4, 3)[:, :, :, :, None, :],
                          (t, 2, g, SSM_GROUP, reps, SSM_STATE)).reshape(t, 2, LANES, LANES)
    c = jnp.stack([c_re, c_im]).astype(_F32).reshape(2, t, g, SSM_GROUP, SSM_STATE)
    ct = jnp.broadcast_to(c.transpose(1, 0, 2, 4, 3)[:, :, :, :, None, :],
                          (t, 2, g, SSM_STATE, g, SSM_GROUP)).reshape(t, 2, STATE_PER_TILE, LANES)
    return rowp, colp, bt, ct


def _kv_kernel(mem_ref, w_ref, o_ref):
    o_ref[...] = _dot(mem_ref[...].astype(_BF16), w_ref[...]).astype(o_ref.dtype)


def _kv_call(mem2d, w_kv):
    rows = mem2d.shape[0]
    return pl.pallas_call(
        _kv_kernel,
        out_shape=jax.ShapeDtypeStruct((rows, 2 * XATTN_WIDTH), _BF16),
        grid=(1,),
        in_specs=[pl.BlockSpec(mem2d.shape, lambda i: (0, 0)),
                  pl.BlockSpec(w_kv.shape, lambda i: (0, 0))],
        out_specs=pl.BlockSpec((rows, 2 * XATTN_WIDTH), lambda i: (0, 0)),
        compiler_params=pltpu.CompilerParams(vmem_limit_bytes=VMEM_LIMIT),
        name="kv",
    )(mem2d, w_kv)


def _s5_kernel(steps_per_seq, x_ref, wu_ref, rowp_ref, colp_ref, bt_ref, ct_ref, d_ref, cast_ref,
               o_ref, cast_out_ref, w1_ref, tm_ref, w3_ref, a8t_ref, a8_ref, u_ref, xp_ref, yi_ref, y_ref,
               zr_ref, zi_ref, sr_ref, si_ref, carry_ref):
    rows = x_ref.shape[0] // PACK

    cast_out_ref[...] = cast_ref[...].astype(cast_out_ref.dtype)

    @pl.when(pl.program_id(0) == 0)
    def _():
        def build(t, c):
            _build_tables(rowp_ref.at[t], colp_ref.at[t], bt_ref.at[t], ct_ref.at[t],
                          w1_ref.at[t], tm_ref.at[t], w3_ref.at[t], a8t_ref.at[t])
            return c
        lax.fori_loop(0, N_LANE_TILES, build, 0)
        for t in range(N_LANE_TILES):
            a8_ref[:, :, t * STATE_PER_TILE:(t + 1) * STATE_PER_TILE] = a8t_ref[t]

    @pl.when(pl.program_id(0) % steps_per_seq == 0)
    def _():
        carry_ref[...] = jnp.zeros_like(carry_ref)

    u = _dot(x_ref[...].astype(_BF16), wu_ref[...])
    for t in range(N_LANE_TILES):
        u_ref[t] = u[:, t * LANES:(t + 1) * LANES]

    for t in range(N_LANE_TILES):
        xp_ref[t] = jnp.concatenate(
            [u_ref[t, pl.ds(j, rows, stride=PACK), :] for j in range(PACK)],
            axis=1).astype(_BF16)
        z = _dot(xp_ref[t], w1_ref[t])
        zr_ref[:, t * STATE_PER_TILE:(t + 1) * STATE_PER_TILE] = z[:, :STATE_PER_TILE]
        zi_ref[:, t * STATE_PER_TILE:(t + 1) * STATE_PER_TILE] = z[:, STATE_PER_TILE:]

    a_r = a8_ref[0]
    a_i = a8_ref[1]
    s_r = carry_ref[0]
    s_i = carry_ref[1]
    seg = rows // N_LANE_TILES
    for t in range(N_LANE_TILES):
        for c in range(PACKED_TILE // TM_PANEL):
            hi = (c + 1) * TM_PANEL
            yi_ref[t, :, c * TM_PANEL:hi] = _dot(xp_ref[t, :, :hi],
                                                 tm_ref[t, :hi, c * TM_PANEL:hi])
        for r in range(t * seg, (t + 1) * seg):
            sr_ref[r:r + 1, :] = s_r
            si_ref[r:r + 1, :] = s_i
            z_r = zr_ref[r:r + 1, :]
            z_i = zi_ref[r:r + 1, :]
            s_r, s_i = a_r * s_r - a_i * s_i + z_r, a_r * s_i + a_i * s_r + z_i
    carry_ref[0] = s_r
    carry_ref[1] = s_i

    for t in range(N_LANE_TILES):
        sl = slice(t * STATE_PER_TILE, (t + 1) * STATE_PER_TILE)
        sprev = jnp.concatenate([sr_ref[:, sl], si_ref[:, sl]], axis=1).astype(_BF16)
        y = yi_ref[t] + _dot(sprev, w3_ref[t])
        for i in range(PACK):
            y_ref[t, pl.ds(i, rows, stride=PACK), :] = y[:, i * LANES:(i + 1) * LANES]
        cols = slice(t * LANES, (t + 1) * LANES)
        o_ref[:, cols] = jax.nn.gelu(y_ref[t] + d_ref[:, cols] * u_ref[t]).astype(o_ref.dtype)


def _s5_call(x2d, w_in_b, tables, d_skip, layer, w_cast, seq_len):
    rowp, colp, bt, ct = tables
    n = x2d.shape[0]
    tokens = S5_ROWS * PACK
    steps = n // tokens
    steps_per_seq = seq_len // tokens
    u_block = (GATE_COLS + 3 * CONV_WIDTH) // SSM_WIDTH
    cast_rows, cast_cols = w_cast.shape[1:]
    cast_block = cast_cols // steps
    assert cast_block * steps == cast_cols and cast_block % LANES == 0
    table = pltpu.VMEM((N_LANE_TILES, PACKED_TILE, PACKED_TILE), _BF16)
    slab = pltpu.VMEM((N_LANE_TILES, tokens, LANES), _F32)
    state = pltpu.VMEM((S5_ROWS, N_STATE), _F32)
    return pl.pallas_call(
        functools.partial(_s5_kernel, steps_per_seq),
        out_shape=(jax.ShapeDtypeStruct((n, SSM_WIDTH), _BF16),
                   jax.ShapeDtypeStruct((cast_rows, cast_cols), _BF16)),
        grid=(steps,),
        in_specs=[pl.BlockSpec((tokens, D_MODEL), lambda i: (i, 0)),
                  pl.BlockSpec((D_MODEL, SSM_WIDTH), lambda i: (0, u_block),
                               pipeline_mode=pl.Buffered(1))]
                 + [_const_spec(c.shape) for c in (rowp, colp, bt, ct)]
                 + [_layer_spec(d_skip, layer),
                    pl.BlockSpec((None, cast_rows, cast_block), lambda i: (layer, 0, i))],
        out_specs=(pl.BlockSpec((tokens, SSM_WIDTH), lambda i: (i, 0)),
                   pl.BlockSpec((cast_rows, cast_block), lambda i: (0, i))),
        scratch_shapes=[table, table, table,
                        pltpu.VMEM((N_LANE_TILES, 2, 1, STATE_PER_TILE), _F32),
                        pltpu.VMEM((2, 1, N_STATE), _F32), slab,
                        pltpu.VMEM((N_LANE_TILES, S5_ROWS, PACKED_TILE), _BF16),
                        pltpu.VMEM((N_LANE_TILES, S5_ROWS, PACKED_TILE), _F32),
                        slab, state, state, state, state,
                        pltpu.VMEM((2, 1, N_STATE), _F32)],
        compiler_params=pltpu.CompilerParams(dimension_semantics=("arbitrary",),
                                             vmem_limit_bytes=VMEM_LIMIT),
        name="s5",
    )(x2d, w_in_b, rowp, colp, bt, ct, d_skip, w_cast)


def _mixer_kernel(steps_per_seq, x_ref, ys_ref, kv_ref, wg_ref, bg_ref, wc_ref, cw_ref,
                  wco_ref, wglu_ref, wq_ref, wxo_ref, wo_ref, g_ref, b_ref, cast_ref,
                  o_ref, cast_out_ref, zbuf_ref):
    tm_ = x_ref.shape[0]

    cast_out_ref[...] = cast_ref[...].astype(cast_out_ref.dtype)

    @pl.when(pl.program_id(0) % steps_per_seq == 0)
    def _():
        zbuf_ref[0:SUBLANES, :] = jnp.zeros((SUBLANES, CONV_WIDTH), _F32)

    def project(r):
        r0 = r * MIX_SUB
        xb = x_ref[r0:r0 + MIX_SUB, :].astype(_BF16)
        q = _dot(xb, wq_ref[...]).astype(_BF16)
        cb, cc, ch = [_dot(xb, wc_ref[:, k * CONV_WIDTH:(k + 1) * CONV_WIDTH]) for k in range(3)]
        scores = []
        for h in range(XATTN_HEADS):
            hs = slice(h * XATTN_HEAD_DIM, (h + 1) * XATTN_HEAD_DIM)
            scores.append(lax.dot_general(q[:, hs], kv_ref[:, hs], (((1,), (1,)), ((), ())),
                                          preferred_element_type=_F32))
        gates = [_dot(xb, wg_ref[:, k * D_MODEL:(k + 1) * D_MODEL]) for k in range(N_BRANCH)]

        z = cc * ch
        zbuf_ref[SUBLANES + r0:SUBLANES + r0 + MIX_SUB, :] = z
        conv = (cw_ref[0:1, :] * zbuf_ref[SUBLANES - 2 + r0:SUBLANES - 2 + r0 + MIX_SUB, :]
                + cw_ref[1:2, :] * zbuf_ref[SUBLANES - 1 + r0:SUBLANES - 1 + r0 + MIX_SUB, :]
                + cw_ref[2:3, :] * z)
        ya_in = (cb * conv).astype(_BF16)

        probs, dens = [], []
        for h in range(XATTN_HEADS):
            s = scores[h] * (XATTN_HEAD_DIM ** -0.5)
            p = jnp.exp(s - jnp.max(s, axis=-1, keepdims=True))
            dens.append(jnp.sum(p, axis=-1, keepdims=True))
            probs.append(p.astype(_BF16))
        return ya_in, probs, dens, gates

    def mix(r, ya_in, probs, dens, gates):
        rows = slice(r * MIX_SUB, (r + 1) * MIX_SUB)

        def gate(k):
            return jax.nn.sigmoid(gates[k] + bg_ref[:, k * D_MODEL:(k + 1) * D_MODEL])

        glu = _dot(ys_ref[rows, :], wglu_ref[...])
        y_a = _dot(ya_in, wco_ref[...])
        heads = []
        for h in range(XATTN_HEADS):
            vs = slice(XATTN_WIDTH + h * XATTN_HEAD_DIM, XATTN_WIDTH + (h + 1) * XATTN_HEAD_DIM)
            heads.append((_dot(probs[h], kv_ref[:, vs]) / dens[h]).astype(_BF16))
        y_c = _dot(jnp.concatenate(heads, axis=1), wxo_ref[...])

        merged = gate(0) * y_a + gate(2) * y_c
        merged += gate(1) * (glu[:, :D_MODEL] * jax.nn.sigmoid(glu[:, D_MODEL:]))
        v = ALPHA * x_ref[rows, :] + _dot(merged.astype(_BF16), wo_ref[...])
        o_ref[rows, :] = _layer_norm(v, g_ref[...], b_ref[...])

    n_chains = tm_ // MIX_SUB
    pending = project(0)
    for r in range(n_chains):
        current = pending
        if r + 1 < n_chains:
            pending = project(r + 1)
        mix(r, *current)

    zbuf_ref[0:SUBLANES, :] = zbuf_ref[tm_:tm_ + SUBLANES, :]


def _col_block_spec(width, start):
    assert start % width == 0
    return pl.BlockSpec((D_MODEL, width), lambda i: (0, start // width),
                        pipeline_mode=pl.Buffered(1))


def _layer_spec(param, layer):
    rest = param.shape[1:]
    if len(rest) == 1:
        return pl.BlockSpec((1,) + rest, lambda *_: (layer, 0), pipeline_mode=pl.Buffered(1))
    zeros = (0,) * len(rest)
    return pl.BlockSpec((None,) + rest, lambda *_: (layer,) + zeros,
                        pipeline_mode=pl.Buffered(1))


def _mixer_call(x2d, ys, kv, w_in_b, b_g, conv_w, w_co, w_glu, w_xo, w_o, ln_g, ln_b,
                layer, w_cast, seq_len):
    n = x2d.shape[0]
    steps = n // MIX_TOKENS
    steps_per_seq = seq_len // MIX_TOKENS
    conv_start = GATE_COLS
    q_start = GATE_COLS + 3 * CONV_WIDTH + SSM_WIDTH
    cast_rows, cast_cols = w_cast.shape[1:]
    cast_block = cast_rows // steps
    assert cast_block * steps == cast_rows and cast_block % (2 * SUBLANES) == 0
    operands = [w_in_b, b_g, w_in_b, conv_w, w_co, w_glu, w_in_b, w_xo, w_o, ln_g, ln_b, w_cast]
    specs = [_col_block_spec(GATE_COLS, 0), _layer_spec(b_g, layer),
             _col_block_spec(3 * CONV_WIDTH, conv_start), _layer_spec(conv_w, layer)]
    specs += [_const_spec(c.shape) for c in (w_co, w_glu)]
    specs += [_col_block_spec(XATTN_WIDTH, q_start)]
    specs += [_const_spec(c.shape) for c in (w_xo, w_o)]
    specs += [_layer_spec(ln_g, layer), _layer_spec(ln_b, layer),
              pl.BlockSpec((None, cast_block, cast_cols), lambda i: (layer, i, 0))]
    return pl.pallas_call(
        functools.partial(_mixer_kernel, steps_per_seq),
        out_shape=(jax.ShapeDtypeStruct((n, D_MODEL), _F32),
                   jax.ShapeDtypeStruct((cast_rows, cast_cols), _BF16)),
        grid=(steps,),
        in_specs=[pl.BlockSpec((MIX_TOKENS, D_MODEL), lambda i: (i, 0)),
                  pl.BlockSpec((MIX_TOKENS, SSM_WIDTH), lambda i: (i, 0)),
                  pl.BlockSpec((MEM_LEN, 2 * XATTN_WIDTH), lambda i: (i // steps_per_seq, 0))]
                 + specs,
        out_specs=(pl.BlockSpec((MIX_TOKENS, D_MODEL), lambda i: (i, 0)),
                   pl.BlockSpec((cast_block, cast_cols), lambda i: (i, 0))),
        scratch_shapes=[pltpu.VMEM((MIX_TOKENS + 2 * SUBLANES, CONV_WIDTH), _F32)],
        compiler_params=pltpu.CompilerParams(dimension_semantics=("arbitrary",),
                                             vmem_limit_bytes=VMEM_LIMIT),
        name="mixers",
    )(x2d, ys, kv, *operands)


def _mlp_kernel(x_ref, wu_ref, bu_ref, wd_ref, bd_ref, g_ref, b_ref, o_ref):
    n_chunks = D_FF // FF_CHUNK
    units = [(r, c) for r in range(MLP_TOKENS // MLP_SUB) for c in range(n_chunks)]
    xb, acc = {}, {}

    def up(r, c):
        if c == 0:
            xb[r] = x_ref[r * MLP_SUB:(r + 1) * MLP_SUB, :].astype(_BF16)
        return _dot(xb[r], wu_ref[:, c * FF_CHUNK:(c + 1) * FF_CHUNK])

    def down(r, c, pre):
        cols = slice(c * FF_CHUNK, (c + 1) * FF_CHUNK)
        h = jnp.maximum(pre + bu_ref[:, cols], 0.0)
        if c == 0:
            acc[r] = ALPHA * x_ref[r * MLP_SUB:(r + 1) * MLP_SUB, :] + bd_ref[...]
        acc[r] += _dot((h * h).astype(_BF16), wd_ref[cols, :])

    pre = up(*units[0])
    for k, (r, c) in enumerate(units):
        nxt = up(*units[k + 1]) if k + 1 < len(units) else None
        down(r, c, pre)
        if c == n_chunks - 1:
            o_ref[r * MLP_SUB:(r + 1) * MLP_SUB, :] = _layer_norm(acc.pop(r), g_ref[...],
                                                                  b_ref[...])
        pre = nxt


def _mlp_call(x1, w_up, b_up, w_down, b_down, ln_g, ln_b, layer):
    n = x1.shape[0]
    consts = [w_up, b_up, w_down, b_down, ln_g, ln_b]
    return pl.pallas_call(
        _mlp_kernel,
        out_shape=jax.ShapeDtypeStruct((n, D_MODEL), _F32),
        grid=(n // MLP_TOKENS,),
        in_specs=[pl.BlockSpec((MLP_TOKENS, D_MODEL), lambda i: (i, 0)),
                  _const_spec(w_up.shape), _layer_spec(b_up, layer),
                  _const_spec(w_down.shape), _layer_spec(b_down, layer),
                  _layer_spec(ln_g, layer), _layer_spec(ln_b, layer)],
        out_specs=pl.BlockSpec((MLP_TOKENS, D_MODEL), lambda i: (i, 0)),
        compiler_params=pltpu.CompilerParams(dimension_semantics=("arbitrary",),
                                             vmem_limit_bytes=VMEM_LIMIT),
        name="mlp",
    )(x1, *consts)


def kernel(x, mem, w_in, b_gate, conv_w, w_conv_out, ssm_lam_re, ssm_lam_im, ssm_log_dt,
           ssm_b_re, ssm_b_im, ssm_c_re, ssm_c_im, ssm_d, w_glu, w_kv, w_xattn_out, w_out,
           ln1_g, ln1_b, w_up, b_up, w_down, b_down, ln2_g, ln2_b):
    bsz, seq, d = x.shape
    assert d == D_MODEL and seq % (PACK * S5_ROWS) == 0 and seq % MIX_TOKENS == 0
    assert w_in.shape[0] == DEPTH == 1
    n = bsz * seq
    l = 0

    w_in_b = w_in[l].astype(_BF16)
    tables = _s5_prep(ssm_lam_re[l], ssm_lam_im[l], ssm_log_dt[l], ssm_b_re[l],
                      ssm_b_im[l], ssm_c_re[l], ssm_c_im[l])

    x2d = x.reshape(n, d)
    kv = _kv_call(mem.reshape(bsz * MEM_LEN, d), w_kv[l].astype(_BF16))
    ys, w_up_b = _s5_call(x2d, w_in_b, tables, ssm_d, l, w_up, seq)
    x1, w_down_b = _mixer_call(x2d, ys, kv, w_in_b, b_gate, conv_w,
                               w_conv_out[l].astype(_BF16), w_glu[l].astype(_BF16),
                               w_xattn_out[l].astype(_BF16), w_out[l].astype(_BF16),
                               ln1_g, ln1_b, l, w_down, seq)
    out = _mlp_call(x1, w_up_b, b_up, w_down_b, b_down, ln2_g, ln2_b, l)
    return out.reshape(bsz, seq, d)
```

```python
import functools
import math

import jax
import jax.numpy as jnp
from jax import lax
from jax.experimental import pallas as pl
from jax.experimental.pallas import tpu as pltpu

D_MODEL = 1024
MEM_LEN = 256
N_BRANCH = 3
CONV_WIDTH = 512
CONV_TAPS = 3
SSM_WIDTH = 512
SSM_GROUP = 16
SSM_GROUPS = 32
SSM_STATE = 64
XATTN_HEADS = 4
XATTN_HEAD_DIM = 128
XATTN_WIDTH = 512
D_FF = 4096
GATE_COLS = N_BRANCH * D_MODEL
DEPTH = 1
ALPHA = (2.0 * DEPTH) ** 0.25
LN_EPS = 1e-5

LANES = 128
SUBLANES = 8
PACK = SUBLANES
GROUPS_PER_TILE = LANES // SSM_GROUP
N_LANE_TILES = SSM_WIDTH // LANES
STATE_PER_TILE = GROUPS_PER_TILE * SSM_STATE
N_STATE = SSM_GROUPS * SSM_STATE
PACKED_TILE = PACK * LANES

S5_ROWS = 128
TM_PANEL = 256
MIX_TOKENS = 1024
MIX_SUB = 256
MLP_TOKENS = 1024
MLP_SUB = 256
FF_CHUNK = 1024
VMEM_LIMIT = 60 * 1024 * 1024

_BF16 = jnp.bfloat16
_F32 = jnp.float32


def _const_spec(shape):
    zeros = (0,) * len(shape)
    return pl.BlockSpec(shape, lambda *_: zeros, pipeline_mode=pl.Buffered(1))


def _dot(a, b):
    return jnp.dot(a, b, preferred_element_type=_F32)


def _layer_norm(v, g, b):
    mu = jnp.mean(v, axis=-1, keepdims=True)
    c = v - mu
    var = jnp.mean(c * c, axis=-1, keepdims=True)
    return c * lax.rsqrt(var + LN_EPS) * g + b


def _complex_powers(ar, ai, n):
    out = [(jnp.ones_like(ar), jnp.zeros_like(ai))]
    for _ in range(n):
        pr, pi = out[-1]
        out.append((pr * ar - pi * ai, pr * ai + pi * ar))
    return out


def _build_tables(rowp_ref, colp_ref, bt_ref, ct_ref, w1_ref, tm_ref, w3_ref, a8_ref):
    hp = lax.Precision.HIGHEST
    row_pow = _complex_powers(rowp_ref[0:1, :], rowp_ref[1:2, :], PACK)
    col_pow = _complex_powers(colp_ref[:, 0:1], colp_ref[:, 1:2], PACK)
    zr, zi = rowp_ref[2:3, :], rowp_ref[3:4, :]
    a8_ref[0], a8_ref[1] = row_pow[PACK]
    rows1 = lax.broadcasted_iota(jnp.int32, (LANES, STATE_PER_TILE), 0) // SSM_GROUP
    cols1 = lax.broadcasted_iota(jnp.int32, (LANES, STATE_PER_TILE), 1) // SSM_STATE
    same1 = rows1 == cols1
    rows3 = lax.broadcasted_iota(jnp.int32, (STATE_PER_TILE, LANES), 0) // SSM_STATE
    cols3 = lax.broadcasted_iota(jnp.int32, (STATE_PER_TILE, LANES), 1) // SSM_GROUP
    same3 = rows3 == cols3

    lane_reps = STATE_PER_TILE // LANES
    btr = jnp.concatenate([bt_ref[0]] * lane_reps, axis=1)
    bti = jnp.concatenate([bt_ref[1]] * lane_reps, axis=1)
    bbar = None
    for j in range(PACK):
        qr, qi = row_pow[PACK - 1 - j]
        pr, pi = qr * zr - qi * zi, qr * zi + qi * zr
        sre = jnp.where(same1, pr * btr - pi * bti, 0.0)
        sim = jnp.where(same1, pr * bti + pi * btr, 0.0)
        w1_ref[j * LANES:(j + 1) * LANES, :STATE_PER_TILE] = sre.astype(w1_ref.dtype)
        w1_ref[j * LANES:(j + 1) * LANES, STATE_PER_TILE:] = sim.astype(w1_ref.dtype)
        if j == PACK - 1:
            bbar = jnp.concatenate([sre, sim], axis=1)

    ctr, cti = ct_ref[0], ct_ref[1]
    tker = []
    for q in range(PACK + 1):
        pr, pi = col_pow[q]
        ere = jnp.where(same3, ctr * pr - cti * pi, 0.0)
        eim = jnp.where(same3, ctr * pi + cti * pr, 0.0)
        slab = jnp.concatenate([ere, -eim], axis=0)
        if q >= 1:
            w3_ref[:, (q - 1) * LANES:q * LANES] = slab.astype(w3_ref.dtype)
        if q < PACK:
            tker.append(jnp.dot(bbar, slab, precision=hp,
                                preferred_element_type=_F32).astype(tm_ref.dtype))

    zero = jnp.zeros((LANES, LANES), tm_ref.dtype)
    for j in range(PACK):
        for i in range(PACK):
            tm_ref[j * LANES:(j + 1) * LANES, i * LANES:(i + 1) * LANES] = (
                tker[i - j] if i >= j else zero)


def _s5_prep(lam_re, lam_im, log_dt, b_re, b_im, c_re, c_im):
    lr, li = lam_re.astype(_F32), lam_im.astype(_F32)
    dt = jnp.exp(log_dt.astype(_F32))[:, None]
    mag = jnp.exp(lr * dt)
    ar = mag * jnp.cos(li * dt)
    ai = mag * jnp.sin(li * dt)
    den = lr * lr + li * li
    nr, ni = ar - 1.0, ai
    kr = (nr * lr + ni * li) / den
    ki = (ni * lr - nr * li) / den
    t, g = N_LANE_TILES, GROUPS_PER_TILE

    rowp = jnp.stack([ar, ai, kr, ki]).reshape(4, t, STATE_PER_TILE).transpose(1, 0, 2)
    colp = rowp.transpose(0, 2, 1)
    b = jnp.stack([b_re, b_im]).astype(_F32).reshape(2, t, g, SSM_STATE, SSM_GROUP)
    reps = LANES // SSM_STATE
    bt = jnp.broadcast_to(b.transpose(1, 0, 2, 4, 3)[:, :, :, :, None, :],
                          (t, 2, g, SSM_GROUP, reps, SSM_STATE)).reshape(t, 2, LANES, LANES)
    c = jnp.stack([c_re, c_im]).astype(_F32).reshape(2, t, g, SSM_GROUP, SSM_STATE)
    ct = jnp.broadcast_to(c.transpose(1, 0, 2, 4, 3)[:, :, :, :, None, :],
                          (t, 2, g, SSM_STATE, g, SSM_GROUP)).reshape(t, 2, STATE_PER_TILE, LANES)
    return rowp, colp, bt, ct


def _kv_kernel(mem_ref, w_ref, o_ref):
    o_ref[...] = _dot(mem_ref[...].astype(_BF16), w_ref[...]).astype(o_ref.dtype)


def _kv_call(mem2d, w_kv):
    rows = mem2d.shape[0]
    return pl.pallas_call(
        _kv_kernel,
        out_shape=jax.ShapeDtypeStruct((rows, 2 * XATTN_WIDTH), _BF16),
        grid=(1,),
        in_specs=[pl.BlockSpec(mem2d.shape, lambda i: (0, 0)),
                  pl.BlockSpec(w_kv.shape, lambda i: (0, 0))],
        out_specs=pl.BlockSpec((rows, 2 * XATTN_WIDTH), lambda i: (0, 0)),
        compiler_params=pltpu.CompilerParams(vmem_limit_bytes=VMEM_LIMIT),
        name="kv",
    )(mem2d, w_kv)


def _s5_kernel(steps_per_seq, x_ref, wu_ref, rowp_ref, colp_ref, bt_ref, ct_ref, d_ref, cast_ref,
               o_ref, cast_out_ref, w1_ref, tm_ref, w3_ref, a8t_ref, a8_ref, u_ref, xp_ref, yi_ref, y_ref,
               zr_ref, zi_ref, sr_ref, si_ref, carry_ref):
    rows = x_ref.shape[0] // PACK

    cast_out_ref[...] = cast_ref[...].astype(cast_out_ref.dtype)

    @pl.when(pl.program_id(0) == 0)
    def _():
        def build(t, c):
            _build_tables(rowp_ref.at[t], colp_ref.at[t], bt_ref.at[t], ct_ref.at[t],
                          w1_ref.at[t], tm_ref.at[t], w3_ref.at[t], a8t_ref.at[t])
            return c
        lax.fori_loop(0, N_LANE_TILES, build, 0)
        for t in range(N_LANE_TILES):
            a8_ref[:, :, t * STATE_PER_TILE:(t + 1) * STATE_PER_TILE] = a8t_ref[t]

    @pl.when(pl.program_id(0) % steps_per_seq == 0)
    def _():
        carry_ref[...] = jnp.zeros_like(carry_ref)

    u = _dot(x_ref[...].astype(_BF16), wu_ref[...])
    for t in range(N_LANE_TILES):
        u_ref[t] = u[:, t * LANES:(t + 1) * LANES]

    for t in range(N_LANE_TILES):
        xp_ref[t] = jnp.concatenate(
            [u_ref[t, pl.ds(j, rows, stride=PACK), :] for j in range(PACK)],
            axis=1).astype(_BF16)
        z = _dot(xp_ref[t], w1_ref[t])
        zr_ref[:, t * STATE_PER_TILE:(t + 1) * STATE_PER_TILE] = z[:, :STATE_PER_TILE]
        zi_ref[:, t * STATE_PER_TILE:(t + 1) * STATE_PER_TILE] = z[:, STATE_PER_TILE:]

    a_r = a8_ref[0]
    a_i = a8_ref[1]
    s_r = carry_ref[0]
    s_i = carry_ref[1]
    seg = rows // N_LANE_TILES
    for t in range(N_LANE_TILES):
        for c in range(PACKED_TILE // TM_PANEL):
            hi = (c + 1) * TM_PANEL
            yi_ref[t, :, c * TM_PANEL:hi] = _dot(xp_ref[t, :, :hi],
                                                 tm_ref[t, :hi, c * TM_PANEL:hi])
        for r in range(t * seg, (t + 1) * seg):
            sr_ref[r:r + 1, :] = s_r
            si_ref[r:r + 1, :] = s_i
            z_r = zr_ref[r:r + 1, :]
            z_i = zi_ref[r:r + 1, :]
            s_r, s_i = a_r * s_r - a_i * s_i + z_r, a_r * s_i + a_i * s_r + z_i
    carry_ref[0] = s_r
    carry_ref[1] = s_i

    for t in range(N_LANE_TILES):
        sl = slice(t * STATE_PER_TILE, (t + 1) * STATE_PER_TILE)
        sprev = jnp.concatenate([sr_ref[:, sl], si_ref[:, sl]], axis=1).astype(_BF16)
        y = yi_ref[t] + _dot(sprev, w3_ref[t])
        for i in range(PACK):
            y_ref[t, pl.ds(i, rows, stride=PACK), :] = y[:, i * LANES:(i + 1) * LANES]
        cols = slice(t * LANES, (t + 1) * LANES)
        o_ref[:, cols] = jax.nn.gelu(y_ref[t] + d_ref[:, cols] * u_ref[t]).astype(o_ref.dtype)


def _s5_call(x2d, w_in_b, tables, d_skip, layer, w_cast, seq_len):
    rowp, colp, bt, ct = tables
    n = x2d.shape[0]
    tokens = S5_ROWS * PACK
    steps = n // tokens
    steps_per_seq = seq_len // tokens
    u_block = (GATE_COLS + 3 * CONV_WIDTH) // SSM_WIDTH
    cast_rows, cast_cols = w_cast.shape[1:]
    cast_block = cast_cols // steps
    assert cast_block * steps == cast_cols and cast_block % LANES == 0
    table = pltpu.VMEM((N_LANE_TILES, PACKED_TILE, PACKED_TILE), _BF16)
    slab = pltpu.VMEM((N_LANE_TILES, tokens, LANES), _F32)
    state = pltpu.VMEM((S5_ROWS, N_STATE), _F32)
    return pl.pallas_call(
        functools.partial(_s5_kernel, steps_per_seq),
        out_shape=(jax.ShapeDtypeStruct((n, SSM_WIDTH), _BF16),
                   jax.ShapeDtypeStruct((cast_rows, cast_cols), _BF16)),
        grid=(steps,),
        in_specs=[pl.BlockSpec((tokens, D_MODEL), lambda i: (i, 0)),
                  pl.BlockSpec((D_MODEL, SSM_WIDTH), lambda i: (0, u_block),
                               pipeline_mode=pl.Buffered(1))]
                 + [_const_spec(c.shape) for c in (rowp, colp, bt, ct)]
                 + [_layer_spec(d_skip, layer),
                    pl.BlockSpec((None, cast_rows, cast_block), lambda i: (layer, 0, i))],
        out_specs=(pl.BlockSpec((tokens, SSM_WIDTH), lambda i: (i, 0)),
                   pl.BlockSpec((cast_rows, cast_block), lambda i: (0, i))),
        scratch_shapes=[table, table, table,
                        pltpu.VMEM((N_LANE_TILES, 2, 1, STATE_PER_TILE), _F32),
                        pltpu.VMEM((2, 1, N_STATE), _F32), slab,
                        pltpu.VMEM((N_LANE_TILES, S5_ROWS, PACKED_TILE), _BF16),
                        pltpu.VMEM((N_LANE_TILES, S5_ROWS, PACKED_TILE), _F32),
                        slab, state, state, state, state,
                        pltpu.VMEM((2, 1, N_STATE), _F32)],
        compiler_params=pltpu.CompilerParams(dimension_semantics=("arbitrary",),
                                             vmem_limit_bytes=VMEM_LIMIT),
        name="s5",
    )(x2d, w_in_b, rowp, colp, bt, ct, d_skip, w_cast)


def _mixer_kernel(steps_per_seq, x_ref, ys_ref, kv_ref, wg_ref, bg_ref, wc_ref, cw_ref,
                  wco_ref, wglu_ref, wq_ref, wxo_ref, wo_ref, g_ref, b_ref, cast_ref,
                  o_ref, cast_out_ref, zbuf_ref):
    tm_ = x_ref.shape[0]

    cast_out_ref[...] = cast_ref[...].astype(cast_out_ref.dtype)

    @pl.when(pl.program_id(0) % steps_per_seq == 0)
    def _():
        zbuf_ref[0:SUBLANES, :] = jnp.zeros((SUBLANES, CONV_WIDTH), _F32)

    def project(r):
        r0 = r * MIX_SUB
        xb = x_ref[r0:r0 + MIX_SUB, :].astype(_BF16)
        q = _dot(xb, wq_ref[...]).astype(_BF16)
        cb, cc, ch = [_dot(xb, wc_ref[:, k * CONV_WIDTH:(k + 1) * CONV_WIDTH]) for k in range(3)]
        scores = []
        for h in range(XATTN_HEADS):
            hs = slice(h * XATTN_HEAD_DIM, (h + 1) * XATTN_HEAD_DIM)
            scores.append(lax.dot_general(q[:, hs], kv_ref[:, hs], (((1,), (1,)), ((), ())),
                                          preferred_element_type=_F32))
        gates = [_dot(xb, wg_ref[:, k * D_MODEL:(k + 1) * D_MODEL]) for k in range(N_BRANCH)]

        z = cc * ch
        zbuf_ref[SUBLANES + r0:SUBLANES + r0 + MIX_SUB, :] = z
        conv = (cw_ref[0:1, :] * zbuf_ref[SUBLANES - 2 + r0:SUBLANES - 2 + r0 + MIX_SUB, :]
                + cw_ref[1:2, :] * zbuf_ref[SUBLANES - 1 + r0:SUBLANES - 1 + r0 + MIX_SUB, :]
                + cw_ref[2:3, :] * z)
        ya_in = (cb * conv).astype(_BF16)

        probs, dens = [], []
        for h in range(XATTN_HEADS):
            s = scores[h] * (XATTN_HEAD_DIM ** -0.5)
            p = jnp.exp(s - jnp.max(s, axis=-1, keepdims=True))
            dens.append(jnp.sum(p, axis=-1, keepdims=True))
            probs.append(p.astype(_BF16))
        return ya_in, probs, dens, gates

    def mix(r, ya_in, probs, dens, gates):
        rows = slice(r * MIX_SUB, (r + 1) * MIX_SUB)

        def gate(k):
            return jax.nn.sigmoid(gates[k] + bg_ref[:, k * D_MODEL:(k + 1) * D_MODEL])

        glu = _dot(ys_ref[rows, :], wglu_ref[...])
        y_a = _dot(ya_in, wco_ref[...])
        heads = []
        for h in range(XATTN_HEADS):
            vs = slice(XATTN_WIDTH + h * XATTN_HEAD_DIM, XATTN_WIDTH + (h + 1) * XATTN_HEAD_DIM)
            heads.append((_dot(probs[h], kv_ref[:, vs]) / dens[h]).astype(_BF16))
        y_c = _dot(jnp.concatenate(heads, axis=1), wxo_ref[...])

        merged = gate(0) * y_a + gate(2) * y_c
        merged += gate(1) * (glu[:, :D_MODEL] * jax.nn.sigmoid(glu[:, D_MODEL:]))
        v = ALPHA * x_ref[rows, :] + _dot(merged.astype(_BF16), wo_ref[...])
        o_ref[rows, :] = _layer_norm(v, g_ref[...], b_ref[...])

    n_chains = tm_ // MIX_SUB
    pending = project(0)
    for r in range(n_chains):
        current = pending
        if r + 1 < n_chains:
            pending = project(r + 1)
        mix(r, *current)

    zbuf_ref[0:SUBLANES, :] = zbuf_ref[tm_:tm_ + SUBLANES, :]


def _col_block_spec(width, start):
    assert start % width == 0
    return pl.BlockSpec((D_MODEL, width), lambda i: (0, start // width),
                        pipeline_mode=pl.Buffered(1))


def _layer_spec(param, layer):
    rest = param.shape[1:]
    if len(rest) == 1:
        return pl.BlockSpec((1,) + rest, lambda *_: (layer, 0), pipeline_mode=pl.Buffered(1))
    zeros = (0,) * len(rest)
    return pl.BlockSpec((None,) + rest, lambda *_: (layer,) + zeros,
                        pipeline_mode=pl.Buffered(1))


def _mixer_call(x2d, ys, kv, w_in_b, b_g, conv_w, w_co, w_glu, w_xo, w_o, ln_g, ln_b,
                layer, w_cast, seq_len):
    n = x2d.shape[0]
    steps = n // MIX_TOKENS
    steps_per_seq = seq_len // MIX_TOKENS
    conv_start = GATE_COLS
    q_start = GATE_COLS + 3 * CONV_WIDTH + SSM_WIDTH
    cast_rows, cast_cols = w_cast.shape[1:]
    cast_block = cast_rows // steps
    assert cast_block * steps == cast_rows and cast_block % (2 * SUBLANES) == 0
    operands = [w_in_b, b_g, w_in_b, conv_w, w_co, w_glu, w_in_b, w_xo, w_o, ln_g, ln_b, w_cast]
    specs = [_col_block_spec(GATE_COLS, 0), _layer_spec(b_g, layer),
             _col_block_spec(3 * CONV_WIDTH, conv_start), _layer_spec(conv_w, layer)]
    specs += [_const_spec(c.shape) for c in (w_co, w_glu)]
    specs += [_col_block_spec(XATTN_WIDTH, q_start)]
    specs += [_const_spec(c.shape) for c in (w_xo, w_o)]
    specs += [_layer_spec(ln_g, layer), _layer_spec(ln_b, layer),
              pl.BlockSpec((None, cast_block, cast_cols), lambda i: (layer, i, 0))]
    return pl.pallas_call(
        functools.partial(_mixer_kernel, steps_per_seq),
        out_shape=(jax.ShapeDtypeStruct((n, D_MODEL), _F32),
                   jax.ShapeDtypeStruct((cast_rows, cast_cols), _BF16)),
        grid=(steps,),
        in_specs=[pl.BlockSpec((MIX_TOKENS, D_MODEL), lambda i: (i, 0)),
                  pl.BlockSpec((MIX_TOKENS, SSM_WIDTH), lambda i: (i, 0)),
                  pl.BlockSpec((MEM_LEN, 2 * XATTN_WIDTH), lambda i: (i // steps_per_seq, 0))]
                 + specs,
        out_specs=(pl.BlockSpec((MIX_TOKENS, D_MODEL), lambda i: (i, 0)),
                   pl.BlockSpec((cast_block, cast_cols), lambda i: (i, 0))),
        scratch_shapes=[pltpu.VMEM((MIX_TOKENS + 2 * SUBLANES, CONV_WIDTH), _F32)],
        compiler_params=pltpu.CompilerParams(dimension_semantics=("arbitrary",),
                                             vmem_limit_bytes=VMEM_LIMIT),
        name="mixers",
    )(x2d, ys, kv, *operands)


def _mlp_kernel(x_ref, wu_ref, bu_ref, wd_ref, bd_ref, g_ref, b_ref, o_ref):
    n_chunks = D_FF // FF_CHUNK
    units = [(r, c) for r in range(MLP_TOKENS // MLP_SUB) for c in range(n_chunks)]
    xb, acc = {}, {}

    def up(r, c):
        if c == 0:
            xb[r] = x_ref[r * MLP_SUB:(r + 1) * MLP_SUB, :].astype(_BF16)
        return _dot(xb[r], wu_ref[:, c * FF_CHUNK:(c + 1) * FF_CHUNK])

    def down(r, c, pre):
        cols = slice(c * FF_CHUNK, (c + 1) * FF_CHUNK)
        h = jnp.maximum(pre + bu_ref[:, cols], 0.0)
        if c == 0:
            acc[r] = ALPHA * x_ref[r * MLP_SUB:(r + 1) * MLP_SUB, :] + bd_ref[...]
        acc[r] += _dot((h * h).astype(_BF16), wd_ref[cols, :])

    pre = up(*units[0])
    for k, (r, c) in enumerate(units):
        nxt = up(*units[k + 1]) if k + 1 < len(units) else None
        down(r, c, pre)
        if c == n_chunks - 1:
            o_ref[r * MLP_SUB:(r + 1) * MLP_SUB, :] = _layer_norm(acc.pop(r), g_ref[...],
                                                                  b_ref[...])
        pre = nxt


def _mlp_call(x1, w_up, b_up, w_down, b_down, ln_g, ln_b, layer):
    n = x1.shape[0]
    consts = [w_up, b_up, w_down, b_down, ln_g, ln_b]
    return pl.pallas_call(
        _mlp_kernel,
        out_shape=jax.ShapeDtypeStruct((n, D_MODEL), _F32),
        grid=(n // MLP_TOKENS,),
        in_specs=[pl.BlockSpec((MLP_TOKENS, D_MODEL), lambda i: (i, 0)),
                  _const_spec(w_up.shape), _layer_spec(b_up, layer),
                  _const_spec(w_down.shape), _layer_spec(b_down, layer),
                  _layer_spec(ln_g, layer), _layer_spec(ln_b, layer)],
        out_specs=pl.BlockSpec((MLP_TOKENS, D_MODEL), lambda i: (i, 0)),
        compiler_params=pltpu.CompilerParams(dimension_semantics=("arbitrary",),
                                             vmem_limit_bytes=VMEM_LIMIT),
        name="mlp",
    )(x1, *consts)


def kernel(x, mem, w_in, b_gate, conv_w, w_conv_out, ssm_lam_re, ssm_lam_im, ssm_log_dt,
           ssm_b_re, ssm_b_im, ssm_c_re, ssm_c_im, ssm_d, w_glu, w_kv, w_xattn_out, w_out,
           ln1_g, ln1_b, w_up, b_up, w_down, b_down, ln2_g, ln2_b):
    bsz, seq, d = x.shape
    assert d == D_MODEL and seq % (PACK * S5_ROWS) == 0 and seq % MIX_TOKENS == 0
    assert w_in.shape[0] == DEPTH == 1
    n = bsz * seq
    l = 0

    w_in_b = w_in[l].astype(_BF16)
    tables = _s5_prep(ssm_lam_re[l], ssm_lam_im[l], ssm_log_dt[l], ssm_b_re[l],
                      ssm_b_im[l], ssm_c_re[l], ssm_c_im[l])

    x2d = x.reshape(n, d)
    kv = _kv_call(mem.reshape(bsz * MEM_LEN, d), w_kv[l].astype(_BF16))
    ys, w_up_b = _s5_call(x2d, w_in_b, tables, ssm_d, l, w_up, seq)
    x1, w_down_b = _mixer_call(x2d, ys, kv, w_in_b, b_gate, conv_w,
                               w_conv_out[l].astype(_BF16), w_glu[l].astype(_BF16),
                               w_xattn_out[l].astype(_BF16), w_out[l].astype(_BF16),
                               ln1_g, ln1_b, l, w_down, seq)
    out = _mlp_call(x1, w_up_b, b_up, w_down_b, b_down, ln2_g, ln2_b, l)
    return out.reshape(bsz, seq, d)
```

```python
import functools
import math

import jax
import jax.numpy as jnp
from jax import lax
from jax.experimental import pallas as pl
from jax.experimental.pallas import tpu as pltpu

D_MODEL = 1024
MEM_LEN = 256
N_BRANCH = 3
CONV_WIDTH = 512
CONV_TAPS = 3
SSM_WIDTH = 512
SSM_GROUP = 16
SSM_GROUPS = 32
SSM_STATE = 64
XATTN_HEADS = 4
XATTN_HEAD_DIM = 128
XATTN_WIDTH = 512
D_FF = 4096
GATE_COLS = N_BRANCH * D_MODEL
DEPTH = 1
ALPHA = (2.0 * DEPTH) ** 0.25
LN_EPS = 1e-5

LANES = 128
SUBLANES = 8
PACK = SUBLANES
GROUPS_PER_TILE = LANES // SSM_GROUP
N_LANE_TILES = SSM_WIDTH // LANES
STATE_PER_TILE = GROUPS_PER_TILE * SSM_STATE
N_STATE = SSM_GROUPS * SSM_STATE
PACKED_TILE = PACK * LANES

S5_ROWS = 128
TM_PANEL = 256
MIX_TOKENS = 512
MIX_SUB = 256
MLP_TOKENS = 512
MLP_SUB = 256
FF_CHUNK = 1024
VMEM_LIMIT = 60 * 1024 * 1024

_BF16 = jnp.bfloat16
_F32 = jnp.float32


def _const_spec(shape):
    zeros = (0,) * len(shape)
    return pl.BlockSpec(shape, lambda *_: zeros, pipeline_mode=pl.Buffered(1))


def _dot(a, b):
    return jnp.dot(a, b, preferred_element_type=_F32)


def _layer_norm(v, g, b):
    mu = jnp.mean(v, axis=-1, keepdims=True)
    c = v - mu
    var = jnp.mean(c * c, axis=-1, keepdims=True)
    return c * lax.rsqrt(var + LN_EPS) * g + b


def _complex_powers(ar, ai, n):
    out = [(jnp.ones_like(ar), jnp.zeros_like(ai))]
    for _ in range(n):
        pr, pi = out[-1]
        out.append((pr * ar - pi * ai, pr * ai + pi * ar))
    return out


def _build_tables(rowp_ref, colp_ref, bt_ref, ct_ref, w1_ref, tm_ref, w3_ref, a8_ref):
    hp = lax.Precision.HIGHEST
    row_pow = _complex_powers(rowp_ref[0:1, :], rowp_ref[1:2, :], PACK)
    col_pow = _complex_powers(colp_ref[:, 0:1], colp_ref[:, 1:2], PACK)
    zr, zi = rowp_ref[2:3, :], rowp_ref[3:4, :]
    a8_ref[0], a8_ref[1] = row_pow[PACK]
    rows1 = lax.broadcasted_iota(jnp.int32, (LANES, STATE_PER_TILE), 0) // SSM_GROUP
    cols1 = lax.broadcasted_iota(jnp.int32, (LANES, STATE_PER_TILE), 1) // SSM_STATE
    same1 = rows1 == cols1
    rows3 = lax.broadcasted_iota(jnp.int32, (STATE_PER_TILE, LANES), 0) // SSM_STATE
    cols3 = lax.broadcasted_iota(jnp.int32, (STATE_PER_TILE, LANES), 1) // SSM_GROUP
    same3 = rows3 == cols3

    lane_reps = STATE_PER_TILE // LANES
    btr = jnp.concatenate([bt_ref[0]] * lane_reps, axis=1)
    bti = jnp.concatenate([bt_ref[1]] * lane_reps, axis=1)
    bbar = None
    for j in range(PACK):
        qr, qi = row_pow[PACK - 1 - j]
        pr, pi = qr * zr - qi * zi, qr * zi + qi * zr
        sre = jnp.where(same1, pr * btr - pi * bti, 0.0)
        sim = jnp.where(same1, pr * bti + pi * btr, 0.0)
        w1_ref[j * LANES:(j + 1) * LANES, :STATE_PER_TILE] = sre.astype(w1_ref.dtype)
        w1_ref[j * LANES:(j + 1) * LANES, STATE_PER_TILE:] = sim.astype(w1_ref.dtype)
        if j == PACK - 1:
            bbar = jnp.concatenate([sre, sim], axis=1)

    ctr, cti = ct_ref[0], ct_ref[1]
    tker = []
    for q in range(PACK + 1):
        pr, pi = col_pow[q]
        ere = jnp.where(same3, ctr * pr - cti * pi, 0.0)
        eim = jnp.where(same3, ctr * pi + cti * pr, 0.0)
        slab = jnp.concatenate([ere, -eim], axis=0)
        if q >= 1:
            w3_ref[:, (q - 1) * LANES:q * LANES] = slab.astype(w3_ref.dtype)
        if q < PACK:
            tker.append(jnp.dot(bbar, slab, precision=hp,
                                preferred_element_type=_F32).astype(tm_ref.dtype))

    zero = jnp.zeros((LANES, LANES), tm_ref.dtype)
    for j in range(PACK):
        for i in range(PACK):
            tm_ref[j * LANES:(j + 1) * LANES, i * LANES:(i + 1) * LANES] = (
                tker[i - j] if i >= j else zero)


def _s5_prep(lam_re, lam_im, log_dt, b_re, b_im, c_re, c_im):
    lr, li = lam_re.astype(_F32), lam_im.astype(_F32)
    dt = jnp.exp(log_dt.astype(_F32))[:, None]
    mag = jnp.exp(lr * dt)
    ar = mag * jnp.cos(li * dt)
    ai = mag * jnp.sin(li * dt)
    den = lr * lr + li * li
    nr, ni = ar - 1.0, ai
    kr = (nr * lr + ni * li) / den
    ki = (ni * lr - nr * li) / den
    t, g = N_LANE_TILES, GROUPS_PER_TILE

    rowp = jnp.stack([ar, ai, kr, ki]).reshape(4, t, STATE_PER_TILE).transpose(1, 0, 2)
    colp = rowp.transpose(0, 2, 1)
    b = jnp.stack([b_re, b_im]).astype(_F32).reshape(2, t, g, SSM_STATE, SSM_GROUP)
    reps = LANES // SSM_STATE
    bt = jnp.broadcast_to(b.transpose(1, 0, 2, 4, 3)[:, :, :, :, None, :],
                          (t, 2, g, SSM_GROUP, reps, SSM_STATE)).reshape(t, 2, LANES, LANES)
    c = jnp.stack([c_re, c_im]).astype(_F32).reshape(2, t, g, SSM_GROUP, SSM_STATE)
    ct = jnp.broadcast_to(c.transpose(1, 0, 2, 4, 3)[:, :, :, :, None, :],
                          (t, 2, g, SSM_STATE, g, SSM_GROUP)).reshape(t, 2, STATE_PER_TILE, LANES)
    return rowp, colp, bt, ct


def _kv_kernel(mem_ref, w_ref, o_ref):
    o_ref[...] = _dot(mem_ref[...].astype(_BF16), w_ref[...]).astype(o_ref.dtype)


def _kv_call(mem2d, w_kv):
    rows = mem2d.shape[0]
    return pl.pallas_call(
        _kv_kernel,
        out_shape=jax.ShapeDtypeStruct((rows, 2 * XATTN_WIDTH), _BF16),
        grid=(1,),
        in_specs=[pl.BlockSpec(mem2d.shape, lambda i: (0, 0)),
                  pl.BlockSpec(w_kv.shape, lambda i: (0, 0))],
        out_specs=pl.BlockSpec((rows, 2 * XATTN_WIDTH), lambda i: (0, 0)),
        compiler_params=pltpu.CompilerParams(vmem_limit_bytes=VMEM_LIMIT),
        name="kv",
    )(mem2d, w_kv)


def _s5_kernel(steps_per_seq, x_ref, wu_ref, rowp_ref, colp_ref, bt_ref, ct_ref, d_ref, cast_ref,
               o_ref, cast_out_ref, w1_ref, tm_ref, w3_ref, a8t_ref, a8_ref, u_ref, xp_ref, yi_ref, y_ref,
               zr_ref, zi_ref, sr_ref, si_ref, carry_ref):
    rows = x_ref.shape[0] // PACK

    cast_out_ref[...] = cast_ref[...].astype(cast_out_ref.dtype)

    @pl.when(pl.program_id(0) == 0)
    def _():
        def build(t, c):
            _build_tables(rowp_ref.at[t], colp_ref.at[t], bt_ref.at[t], ct_ref.at[t],
                          w1_ref.at[t], tm_ref.at[t], w3_ref.at[t], a8t_ref.at[t])
            return c
        lax.fori_loop(0, N_LANE_TILES, build, 0)
        for t in range(N_LANE_TILES):
            a8_ref[:, :, t * STATE_PER_TILE:(t + 1) * STATE_PER_TILE] = a8t_ref[t]

    @pl.when(pl.program_id(0) % steps_per_seq == 0)
    def _():
        carry_ref[...] = jnp.zeros_like(carry_ref)

    u = _dot(x_ref[...].astype(_BF16), wu_ref[...])
    for t in range(N_LANE_TILES):
        u_ref[t] = u[:, t * LANES:(t + 1) * LANES]

    for t in range(N_LANE_TILES):
        xp_ref[t] = jnp.concatenate(
            [u_ref[t, pl.ds(j, rows, stride=PACK), :] for j in range(PACK)],
            axis=1).astype(_BF16)
        z = _dot(xp_ref[t], w1_ref[t])
        zr_ref[:, t * STATE_PER_TILE:(t + 1) * STATE_PER_TILE] = z[:, :STATE_PER_TILE]
        zi_ref[:, t * STATE_PER_TILE:(t + 1) * STATE_PER_TILE] = z[:, STATE_PER_TILE:]

    a_r = a8_ref[0]
    a_i = a8_ref[1]
    s_r = carry_ref[0]
    s_i = carry_ref[1]
    seg = rows // N_LANE_TILES
    for t in range(N_LANE_TILES):
        for c in range(PACKED_TILE // TM_PANEL):
            hi = (c + 1) * TM_PANEL
            yi_ref[t, :, c * TM_PANEL:hi] = _dot(xp_ref[t, :, :hi],
                                                 tm_ref[t, :hi, c * TM_PANEL:hi])
        for r in range(t * seg, (t + 1) * seg):
            sr_ref[r:r + 1, :] = s_r
            si_ref[r:r + 1, :] = s_i
            z_r = zr_ref[r:r + 1, :]
            z_i = zi_ref[r:r + 1, :]
            s_r, s_i = a_r * s_r - a_i * s_i + z_r, a_r * s_i + a_i * s_r + z_i
    carry_ref[0] = s_r
    carry_ref[1] = s_i

    for t in range(N_LANE_TILES):
        sl = slice(t * STATE_PER_TILE, (t + 1) * STATE_PER_TILE)
        sprev = jnp.concatenate([sr_ref[:, sl], si_ref[:, sl]], axis=1).astype(_BF16)
        y = yi_ref[t] + _dot(sprev, w3_ref[t])
        for i in range(PACK):
            y_ref[t, pl.ds(i, rows, stride=PACK), :] = y[:, i * LANES:(i + 1) * LANES]
        cols = slice(t * LANES, (t + 1) * LANES)
        o_ref[:, cols] = jax.nn.gelu(y_ref[t] + d_ref[:, cols] * u_ref[t]).astype(o_ref.dtype)


def _s5_call(x2d, w_in_b, tables, d_skip, layer, w_cast, seq_len):
    rowp, colp, bt, ct = tables
    n = x2d.shape[0]
    tokens = S5_ROWS * PACK
    steps = n // tokens
    steps_per_seq = seq_len // tokens
    u_block = (GATE_COLS + 3 * CONV_WIDTH) // SSM_WIDTH
    cast_rows, cast_cols = w_cast.shape[1:]
    cast_block = cast_cols // steps
    assert cast_block * steps == cast_cols and cast_block % LANES == 0
    table = pltpu.VMEM((N_LANE_TILES, PACKED_TILE, PACKED_TILE), _BF16)
    slab = pltpu.VMEM((N_LANE_TILES, tokens, LANES), _F32)
    state = pltpu.VMEM((S5_ROWS, N_STATE), _F32)
    return pl.pallas_call(
        functools.partial(_s5_kernel, steps_per_seq),
        out_shape=(jax.ShapeDtypeStruct((n, SSM_WIDTH), _BF16),
                   jax.ShapeDtypeStruct((cast_rows, cast_cols), _BF16)),
        grid=(steps,),
        in_specs=[pl.BlockSpec((tokens, D_MODEL), lambda i: (i, 0)),
                  pl.BlockSpec((D_MODEL, SSM_WIDTH), lambda i: (0, u_block),
                               pipeline_mode=pl.Buffered(1))]
                 + [_const_spec(c.shape) for c in (rowp, colp, bt, ct)]
                 + [_layer_spec(d_skip, layer),
                    pl.BlockSpec((None, cast_rows, cast_block), lambda i: (layer, 0, i))],
        out_specs=(pl.BlockSpec((tokens, SSM_WIDTH), lambda i: (i, 0)),
                   pl.BlockSpec((cast_rows, cast_block), lambda i: (0, i))),
        scratch_shapes=[table, table, table,
                        pltpu.VMEM((N_LANE_TILES, 2, 1, STATE_PER_TILE), _F32),
                        pltpu.VMEM((2, 1, N_STATE), _F32), slab,
                        pltpu.VMEM((N_LANE_TILES, S5_ROWS, PACKED_TILE), _BF16),
                        pltpu.VMEM((N_LANE_TILES, S5_ROWS, PACKED_TILE), _F32),
                        slab, state, state, state, state,
                        pltpu.VMEM((2, 1, N_STATE), _F32)],
        compiler_params=pltpu.CompilerParams(dimension_semantics=("arbitrary",),
                                             vmem_limit_bytes=VMEM_LIMIT),
        name="s5",
    )(x2d, w_in_b, rowp, colp, bt, ct, d_skip, w_cast)


def _mixer_kernel(steps_per_seq, x_ref, ys_ref, kv_ref, wg_ref, bg_ref, wc_ref, cw_ref,
                  wco_ref, wglu_ref, wq_ref, wxo_ref, wo_ref, g_ref, b_ref, cast_ref,
                  o_ref, cast_out_ref, zbuf_ref):
    tm_ = x_ref.shape[0]

    cast_out_ref[...] = cast_ref[...].astype(cast_out_ref.dtype)

    @pl.when(pl.program_id(0) % steps_per_seq == 0)
    def _():
        zbuf_ref[0:SUBLANES, :] = jnp.zeros((SUBLANES, CONV_WIDTH), _F32)

    def project(r):
        r0 = r * MIX_SUB
        xb = x_ref[r0:r0 + MIX_SUB, :].astype(_BF16)
        q = _dot(xb, wq_ref[...]).astype(_BF16)
        cb, cc, ch = [_dot(xb, wc_ref[:, k * CONV_WIDTH:(k + 1) * CONV_WIDTH]) for k in range(3)]
        scores = []
        for h in range(XATTN_HEADS):
            hs = slice(h * XATTN_HEAD_DIM, (h + 1) * XATTN_HEAD_DIM)
            scores.append(lax.dot_general(q[:, hs], kv_ref[:, hs], (((1,), (1,)), ((), ())),
                                          preferred_element_type=_F32))
        gates = [_dot(xb, wg_ref[:, k * D_MODEL:(k + 1) * D_MODEL]) for k in range(N_BRANCH)]

        z = cc * ch
        zbuf_ref[SUBLANES + r0:SUBLANES + r0 + MIX_SUB, :] = z
        conv = (cw_ref[0:1, :] * zbuf_ref[SUBLANES - 2 + r0:SUBLANES - 2 + r0 + MIX_SUB, :]
                + cw_ref[1:2, :] * zbuf_ref[SUBLANES - 1 + r0:SUBLANES - 1 + r0 + MIX_SUB, :]
                + cw_ref[2:3, :] * z)
        ya_in = (cb * conv).astype(_BF16)

        probs, dens = [], []
        for h in range(XATTN_HEADS):
            s = scores[h] * (XATTN_HEAD_DIM ** -0.5)
            p = jnp.exp(s - jnp.max(s, axis=-1, keepdims=True))
            dens.append(jnp.sum(p, axis=-1, keepdims=True))
            probs.append(p.astype(_BF16))
        return ya_in, probs, dens, gates

    def mix(r, ya_in, probs, dens, gates):
        rows = slice(r * MIX_SUB, (r + 1) * MIX_SUB)

        def gate(k):
            return jax.nn.sigmoid(gates[k] + bg_ref[:, k * D_MODEL:(k + 1) * D_MODEL])

        glu = _dot(ys_ref[rows, :], wglu_ref[...])
        y_a = _dot(ya_in, wco_ref[...])
        heads = []
        for h in range(XATTN_HEADS):
            vs = slice(XATTN_WIDTH + h * XATTN_HEAD_DIM, XATTN_WIDTH + (h + 1) * XATTN_HEAD_DIM)
            heads.append((_dot(probs[h], kv_ref[:, vs]) / dens[h]).astype(_BF16))
        y_c = _dot(jnp.concatenate(heads, axis=1), wxo_ref[...])

        merged = gate(0) * y_a + gate(2) * y_c
        merged += gate(1) * (glu[:, :D_MODEL] * jax.nn.sigmoid(glu[:, D_MODEL:]))
        v = ALPHA * x_ref[rows, :] + _dot(merged.astype(_BF16), wo_ref[...])
        o_ref[rows, :] = _layer_norm(v, g_ref[...], b_ref[...])

    n_chains = tm_ // MIX_SUB
    pending = project(0)
    for r in range(n_chains):
        current = pending
        if r + 1 < n_chains:
            pending = project(r + 1)
        mix(r, *current)

    zbuf_ref[0:SUBLANES, :] = zbuf_ref[tm_:tm_ + SUBLANES, :]


def _col_block_spec(width, start):
    assert start % width == 0
    return pl.BlockSpec((D_MODEL, width), lambda i: (0, start // width),
                        pipeline_mode=pl.Buffered(1))


def _layer_spec(param, layer):
    rest = param.shape[1:]
    if len(rest) == 1:
        return pl.BlockSpec((1,) + rest, lambda *_: (layer, 0), pipeline_mode=pl.Buffered(1))
    zeros = (0,) * len(rest)
    return pl.BlockSpec((None,) + rest, lambda *_: (layer,) + zeros,
                        pipeline_mode=pl.Buffered(1))


def _mixer_call(x2d, ys, kv, w_in_b, b_g, conv_w, w_co, w_glu, w_xo, w_o, ln_g, ln_b,
                layer, w_cast, seq_len):
    n = x2d.shape[0]
    steps = n // MIX_TOKENS
    steps_per_seq = seq_len // MIX_TOKENS
    conv_start = GATE_COLS
    q_start = GATE_COLS + 3 * CONV_WIDTH + SSM_WIDTH
    cast_rows, cast_cols = w_cast.shape[1:]
    cast_block = cast_rows // steps
    assert cast_block * steps == cast_rows and cast_block % (2 * SUBLANES) == 0
    operands = [w_in_b, b_g, w_in_b, conv_w, w_co, w_glu, w_in_b, w_xo, w_o, ln_g, ln_b, w_cast]
    specs = [_col_block_spec(GATE_COLS, 0), _layer_spec(b_g, layer),
             _col_block_spec(3 * CONV_WIDTH, conv_start), _layer_spec(conv_w, layer)]
    specs += [_const_spec(c.shape) for c in (w_co, w_glu)]
    specs += [_col_block_spec(XATTN_WIDTH, q_start)]
    specs += [_const_spec(c.shape) for c in (w_xo, w_o)]
    specs += [_layer_spec(ln_g, layer), _layer_spec(ln_b, layer),
              pl.BlockSpec((None, cast_block, cast_cols), lambda i: (layer, i, 0))]
    return pl.pallas_call(
        functools.partial(_mixer_kernel, steps_per_seq),
        out_shape=(jax.ShapeDtypeStruct((n, D_MODEL), _F32),
                   jax.ShapeDtypeStruct((cast_rows, cast_cols), _BF16)),
        grid=(steps,),
        in_specs=[pl.BlockSpec((MIX_TOKENS, D_MODEL), lambda i: (i, 0)),
                  pl.BlockSpec((MIX_TOKENS, SSM_WIDTH), lambda i: (i, 0)),
                  pl.BlockSpec((MEM_LEN, 2 * XATTN_WIDTH), lambda i: (i // steps_per_seq, 0))]
                 + specs,
        out_specs=(pl.BlockSpec((MIX_TOKENS, D_MODEL), lambda i: (i, 0)),
                   pl.BlockSpec((cast_block, cast_cols), lambda i: (i, 0))),
        scratch_shapes=[pltpu.VMEM((MIX_TOKENS + 2 * SUBLANES, CONV_WIDTH), _F32)],
        compiler_params=pltpu.CompilerParams(dimension_semantics=("arbitrary",),
                                             vmem_limit_bytes=VMEM_LIMIT),
        name="mixers",
    )(x2d, ys, kv, *operands)


def _mlp_kernel(x_ref, wu_ref, bu_ref, wd_ref, bd_ref, g_ref, b_ref, o_ref):
    n_chunks = D_FF // FF_CHUNK
    units = [(r, c) for r in range(MLP_TOKENS // MLP_SUB) for c in range(n_chunks)]
    xb, acc = {}, {}

    def up(r, c):
        if c == 0:
            xb[r] = x_ref[r * MLP_SUB:(r + 1) * MLP_SUB, :].astype(_BF16)
        return _dot(xb[r], wu_ref[:, c * FF_CHUNK:(c + 1) * FF_CHUNK])

    def down(r, c, pre):
        cols = slice(c * FF_CHUNK, (c + 1) * FF_CHUNK)
        h = jnp.maximum(pre + bu_ref[:, cols], 0.0)
        if c == 0:
            acc[r] = ALPHA * x_ref[r * MLP_SUB:(r + 1) * MLP_SUB, :] + bd_ref[...]
        acc[r] += _dot((h * h).astype(_BF16), wd_ref[cols, :])

    pre = up(*units[0])
    for k, (r, c) in enumerate(units):
        nxt = up(*units[k + 1]) if k + 1 < len(units) else None
        down(r, c, pre)
        if c == n_chunks - 1:
            o_ref[r * MLP_SUB:(r + 1) * MLP_SUB, :] = _layer_norm(acc.pop(r), g_ref[...],
                                                                  b_ref[...])
        pre = nxt


def _mlp_call(x1, w_up, b_up, w_down, b_down, ln_g, ln_b, layer):
    n = x1.shape[0]
    consts = [w_up, b_up, w_down, b_down, ln_g, ln_b]
    return pl.pallas_call(
        _mlp_kernel,
        out_shape=jax.ShapeDtypeStruct((n, D_MODEL), _F32),
        grid=(n // MLP_TOKENS,),
        in_specs=[pl.BlockSpec((MLP_TOKENS, D_MODEL), lambda i: (i, 0)),
                  _const_spec(w_up.shape), _layer_spec(b_up, layer),
                  _const_spec(w_down.shape), _layer_spec(b_down, layer),
                  _layer_spec(ln_g, layer), _layer_spec(ln_b, layer)],
        out_specs=pl.BlockSpec((MLP_TOKENS, D_MODEL), lambda i: (i, 0)),
        compiler_params=pltpu.CompilerParams(dimension_semantics=("arbitrary",),
                                             vmem_limit_bytes=VMEM_LIMIT),
        name="mlp",
    )(x1, *consts)


def kernel(x, mem, w_in, b_gate, conv_w, w_conv_out, ssm_lam_re, ssm_lam_im, ssm_log_dt,
           ssm_b_re, ssm_b_im, ssm_c_re, ssm_c_im, ssm_d, w_glu, w_kv, w_xattn_out, w_out,
           ln1_g, ln1_b, w_up, b_up, w_down, b_down, ln2_g, ln2_b):
    bsz, seq, d = x.shape
    assert d == D_MODEL and seq % (PACK * S5_ROWS) == 0 and seq % MIX_TOKENS == 0
    assert w_in.shape[0] == DEPTH == 1
    n = bsz * seq
    l = 0

    w_in_b = w_in[l].astype(_BF16)
    tables = _s5_prep(ssm_lam_re[l], ssm_lam_im[l], ssm_log_dt[l], ssm_b_re[l],
                      ssm_b_im[l], ssm_c_re[l], ssm_c_im[l])

    x2d = x.reshape(n, d)
    kv = _kv_call(mem.reshape(bsz * MEM_LEN, d), w_kv[l].astype(_BF16))
    ys, w_up_b = _s5_call(x2d, w_in_b, tables, ssm_d, l, w_up, seq)
    x1, w_down_b = _mixer_call(x2d, ys, kv, w_in_b, b_gate, conv_w,
                               w_conv_out[l].astype(_BF16), w_glu[l].astype(_BF16),
                               w_xattn_out[l].astype(_BF16), w_out[l].astype(_BF16),
                               ln1_g, ln1_b, l, w_down, seq)
    out = _mlp_call(x1, w_up_b, b_up, w_down_b, b_down, ln2_g, ln2_b, l)
    return out.reshape(bsz, seq, d)
```

```python
import functools
import math

import jax
import jax.numpy as jnp
from jax import lax
from jax.experimental import pallas as pl
from jax.experimental.pallas import tpu as pltpu

D_MODEL = 1024
MEM_LEN = 256
N_BRANCH = 3
CONV_WIDTH = 512
CONV_TAPS = 3
SSM_WIDTH = 512
SSM_GROUP = 16
SSM_GROUPS = 32
SSM_STATE = 64
XATTN_HEADS = 4
XATTN_HEAD_DIM = 128
XATTN_WIDTH = 512
D_FF = 4096
GATE_COLS = N_BRANCH * D_MODEL
DEPTH = 1
ALPHA = (2.0 * DEPTH) ** 0.25
LN_EPS = 1e-5

LANES = 128
SUBLANES = 8
PACK = SUBLANES
GROUPS_PER_TILE = LANES // SSM_GROUP
N_LANE_TILES = SSM_WIDTH // LANES
STATE_PER_TILE = GROUPS_PER_TILE * SSM_STATE
N_STATE = SSM_GROUPS * SSM_STATE
PACKED_TILE = PACK * LANES

S5_ROWS = 128
TM_PANEL = 256
MIX_TOKENS = 512
MIX_SUB = 256
MLP_TOKENS = 1024
MLP_SUB = 256
FF_CHUNK = 1024
VMEM_LIMIT = 60 * 1024 * 1024

_BF16 = jnp.bfloat16
_F32 = jnp.float32


def _const_spec(shape):
    zeros = (0,) * len(shape)
    return pl.BlockSpec(shape, lambda *_: zeros, pipeline_mode=pl.Buffered(1))


def _dot(a, b):
    return jnp.dot(a, b, preferred_element_type=_F32)


def _layer_norm(v, g, b):
    mu = jnp.mean(v, axis=-1, keepdims=True)
    c = v - mu
    var = jnp.mean(c * c, axis=-1, keepdims=True)
    return c * lax.rsqrt(var + LN_EPS) * g + b


def _complex_powers(ar, ai, n):
    out = [(jnp.ones_like(ar), jnp.zeros_like(ai))]
    for _ in range(n):
        pr, pi = out[-1]
        out.append((pr * ar - pi * ai, pr * ai + pi * ar))
    return out


def _build_tables(rowp_ref, colp_ref, bt_ref, ct_ref, w1_ref, tm_ref, w3_ref, a8_ref):
    hp = lax.Precision.HIGHEST
    row_pow = _complex_powers(rowp_ref[0:1, :], rowp_ref[1:2, :], PACK)
    col_pow = _complex_powers(colp_ref[:, 0:1], colp_ref[:, 1:2], PACK)
    zr, zi = rowp_ref[2:3, :], rowp_ref[3:4, :]
    a8_ref[0], a8_ref[1] = row_pow[PACK]
    rows1 = lax.broadcasted_iota(jnp.int32, (LANES, STATE_PER_TILE), 0) // SSM_GROUP
    cols1 = lax.broadcasted_iota(jnp.int32, (LANES, STATE_PER_TILE), 1) // SSM_STATE
    same1 = rows1 == cols1
    rows3 = lax.broadcasted_iota(jnp.int32, (STATE_PER_TILE, LANES), 0) // SSM_STATE
    cols3 = lax.broadcasted_iota(jnp.int32, (STATE_PER_TILE, LANES), 1) // SSM_GROUP
    same3 = rows3 == cols3

    lane_reps = STATE_PER_TILE // LANES
    btr = jnp.concatenate([bt_ref[0]] * lane_reps, axis=1)
    bti = jnp.concatenate([bt_ref[1]] * lane_reps, axis=1)
    bbar = None
    for j in range(PACK):
        qr, qi = row_pow[PACK - 1 - j]
        pr, pi = qr * zr - qi * zi, qr * zi + qi * zr
        sre = jnp.where(same1, pr * btr - pi * bti, 0.0)
        sim = jnp.where(same1, pr * bti + pi * btr, 0.0)
        w1_ref[j * LANES:(j + 1) * LANES, :STATE_PER_TILE] = sre.astype(w1_ref.dtype)
        w1_ref[j * LANES:(j + 1) * LANES, STATE_PER_TILE:] = sim.astype(w1_ref.dtype)
        if j == PACK - 1:
            bbar = jnp.concatenate([sre, sim], axis=1)

    ctr, cti = ct_ref[0], ct_ref[1]
    tker = []
    for q in range(PACK + 1):
        pr, pi = col_pow[q]
        ere = jnp.where(same3, ctr * pr - cti * pi, 0.0)
        eim = jnp.where(same3, ctr * pi + cti * pr, 0.0)
        slab = jnp.concatenate([ere, -eim], axis=0)
        if q >= 1:
            w3_ref[:, (q - 1) * LANES:q * LANES] = slab.astype(w3_ref.dtype)
        if q < PACK:
            tker.append(jnp.dot(bbar, slab, precision=hp,
                                preferred_element_type=_F32).astype(tm_ref.dtype))

    zero = jnp.zeros((LANES, LANES), tm_ref.dtype)
    for j in range(PACK):
        for i in range(PACK):
            tm_ref[j * LANES:(j + 1) * LANES, i * LANES:(i + 1) * LANES] = (
                tker[i - j] if i >= j else zero)


def _s5_prep(lam_re, lam_im, log_dt, b_re, b_im, c_re, c_im):
    lr, li = lam_re.astype(_F32), lam_im.astype(_F32)
    dt = jnp.exp(log_dt.astype(_F32))[:, None]
    mag = jnp.exp(lr * dt)
    ar = mag * jnp.cos(li * dt)
    ai = mag * jnp.sin(li * dt)
    den = lr * lr + li * li
    nr, ni = ar - 1.0, ai
    kr = (nr * lr + ni * li) / den
    ki = (ni * lr - nr * li) / den
    t, g = N_LANE_TILES, GROUPS_PER_TILE

    rowp = jnp.stack([ar, ai, kr, ki]).reshape(4, t, STATE_PER_TILE).transpose(1, 0, 2)
    colp = rowp.transpose(0, 2, 1)
    b = jnp.stack([b_re, b_im]).astype(_F32).reshape(2, t, g, SSM_STATE, SSM_GROUP)
    reps = LANES // SSM_STATE
    bt = jnp.broadcast_to(b.transpose(1, 0, 2, 4, 3)[:, :, :, :, None, :],
                          (t, 2, g, SSM_GROUP, reps, SSM_STATE)).reshape(t, 2, LANES, LANES)
    c = jnp.stack([c_re, c_im]).astype(_F32).reshape(2, t, g, SSM_GROUP, SSM_STATE)
    ct = jnp.broadcast_to(c.transpose(1, 0, 2, 4, 3)[:, :, :, :, None, :],
                          (t, 2, g, SSM_STATE, g, SSM_GROUP)).reshape(t, 2, STATE_PER_TILE, LANES)
    return rowp, colp, bt, ct


def _kv_kernel(mem_ref, w_ref, o_ref):
    o_ref[...] = _dot(mem_ref[...].astype(_BF16), w_ref[...]).astype(o_ref.dtype)


def _kv_call(mem2d, w_kv):
    rows = mem2d.shape[0]
    return pl.pallas_call(
        _kv_kernel,
        out_shape=jax.ShapeDtypeStruct((rows, 2 * XATTN_WIDTH), _BF16),
        grid=(1,),
        in_specs=[pl.BlockSpec(mem2d.shape, lambda i: (0, 0)),
                  pl.BlockSpec(w_kv.shape, lambda i: (0, 0))],
        out_specs=pl.BlockSpec((rows, 2 * XATTN_WIDTH), lambda i: (0, 0)),
        compiler_params=pltpu.CompilerParams(vmem_limit_bytes=VMEM_LIMIT),
        name="kv",
    )(mem2d, w_kv)


def _s5_kernel(steps_per_seq, x_ref, wu_ref, rowp_ref, colp_ref, bt_ref, ct_ref, d_ref, cast_ref,
               o_ref, cast_out_ref, w1_ref, tm_ref, w3_ref, a8t_ref, a8_ref, u_ref, xp_ref, yi_ref, y_ref,
               zr_ref, zi_ref, sr_ref, si_ref, carry_ref):
    rows = x_ref.shape[0] // PACK

    cast_out_ref[...] = cast_ref[...].astype(cast_out_ref.dtype)

    @pl.when(pl.program_id(0) == 0)
    def _():
        def build(t, c):
            _build_tables(rowp_ref.at[t], colp_ref.at[t], bt_ref.at[t], ct_ref.at[t],
                          w1_ref.at[t], tm_ref.at[t], w3_ref.at[t], a8t_ref.at[t])
            return c
        lax.fori_loop(0, N_LANE_TILES, build, 0)
        for t in range(N_LANE_TILES):
            a8_ref[:, :, t * STATE_PER_TILE:(t + 1) * STATE_PER_TILE] = a8t_ref[t]

    @pl.when(pl.program_id(0) % steps_per_seq == 0)
    def _():
        carry_ref[...] = jnp.zeros_like(carry_ref)

    u = _dot(x_ref[...].astype(_BF16), wu_ref[...])
    for t in range(N_LANE_TILES):
        u_ref[t] = u[:, t * LANES:(t + 1) * LANES]

    for t in range(N_LANE_TILES):
        xp_ref[t] = jnp.concatenate(
            [u_ref[t, pl.ds(j, rows, stride=PACK), :] for j in range(PACK)],
            axis=1).astype(_BF16)
        z = _dot(xp_ref[t], w1_ref[t])
        zr_ref[:, t * STATE_PER_TILE:(t + 1) * STATE_PER_TILE] = z[:, :STATE_PER_TILE]
        zi_ref[:, t * STATE_PER_TILE:(t + 1) * STATE_PER_TILE] = z[:, STATE_PER_TILE:]

    a_r = a8_ref[0]
    a_i = a8_ref[1]
    s_r = carry_ref[0]
    s_i = carry_ref[1]
    seg = rows // N_LANE_TILES
    for t in range(N_LANE_TILES):
        for c in range(PACKED_TILE // TM_PANEL):
            hi = (c + 1) * TM_PANEL
            yi_ref[t, :, c * TM_PANEL:hi] = _dot(xp_ref[t, :, :hi],
                                                 tm_ref[t, :hi, c * TM_PANEL:hi])
        for r in range(t * seg, (t + 1) * seg):
            sr_ref[r:r + 1, :] = s_r
            si_ref[r:r + 1, :] = s_i
            z_r = zr_ref[r:r + 1, :]
            z_i = zi_ref[r:r + 1, :]
            s_r, s_i = a_r * s_r - a_i * s_i + z_r, a_r * s_i + a_i * s_r + z_i
    carry_ref[0] = s_r
    carry_ref[1] = s_i

    for t in range(N_LANE_TILES):
        sl = slice(t * STATE_PER_TILE, (t + 1) * STATE_PER_TILE)
        sprev = jnp.concatenate([sr_ref[:, sl], si_ref[:, sl]], axis=1).astype(_BF16)
        y = yi_ref[t] + _dot(sprev, w3_ref[t])
        for i in range(PACK):
            y_ref[t, pl.ds(i, rows, stride=PACK), :] = y[:, i * LANES:(i + 1) * LANES]
        cols = slice(t * LANES, (t + 1) * LANES)
        o_ref[:, cols] = jax.nn.gelu(y_ref[t] + d_ref[:, cols] * u_ref[t]).astype(o_ref.dtype)


def _s5_call(x2d, w_in_b, tables, d_skip, layer, w_cast, seq_len):
    rowp, colp, bt, ct = tables
    n = x2d.shape[0]
    tokens = S5_ROWS * PACK
    steps = n // tokens
    steps_per_seq = seq_len // tokens
    u_block = (GATE_COLS + 3 * CONV_WIDTH) // SSM_WIDTH
    cast_rows, cast_cols = w_cast.shape[1:]
    cast_block = cast_cols // steps
    assert cast_block * steps == cast_cols and cast_block % LANES == 0
    table = pltpu.VMEM((N_LANE_TILES, PACKED_TILE, PACKED_TILE), _BF16)
    slab = pltpu.VMEM((N_LANE_TILES, tokens, LANES), _F32)
    state = pltpu.VMEM((S5_ROWS, N_STATE), _F32)
    return pl.pallas_call(
        functools.partial(_s5_kernel, steps_per_seq),
        out_shape=(jax.ShapeDtypeStruct((n, SSM_WIDTH), _BF16),
                   jax.ShapeDtypeStruct((cast_rows, cast_cols), _BF16)),
        grid=(steps,),
        in_specs=[pl.BlockSpec((tokens, D_MODEL), lambda i: (i, 0)),
                  pl.BlockSpec((D_MODEL, SSM_WIDTH), lambda i: (0, u_block),
                               pipeline_mode=pl.Buffered(1))]
                 + [_const_spec(c.shape) for c in (rowp, colp, bt, ct)]
                 + [_layer_spec(d_skip, layer),
                    pl.BlockSpec((None, cast_rows, cast_block), lambda i: (layer, 0, i))],
        out_specs=(pl.BlockSpec((tokens, SSM_WIDTH), lambda i: (i, 0)),
                   pl.BlockSpec((cast_rows, cast_block), lambda i: (0, i))),
        scratch_shapes=[table, table, table,
                        pltpu.VMEM((N_LANE_TILES, 2, 1, STATE_PER_TILE), _F32),
                        pltpu.VMEM((2, 1, N_STATE), _F32), slab,
                        pltpu.VMEM((N_LANE_TILES, S5_ROWS, PACKED_TILE), _BF16),
                        pltpu.VMEM((N_LANE_TILES, S5_ROWS, PACKED_TILE), _F32),
                        slab, state, state, state, state,
                        pltpu.VMEM((2, 1, N_STATE), _F32)],
        compiler_params=pltpu.CompilerParams(dimension_semantics=("arbitrary",),
                                             vmem_limit_bytes=VMEM_LIMIT),
        name="s5",
    )(x2d, w_in_b, rowp, colp, bt, ct, d_skip, w_cast)


def _mixer_kernel(steps_per_seq, x_ref, ys_ref, kv_ref, wg_ref, bg_ref, wc_ref, cw_ref,
                  wco_ref, wglu_ref, wq_ref, wxo_ref, wo_ref, g_ref, b_ref, cast_ref,
                  o_ref, cast_out_ref, zbuf_ref):
    tm_ = x_ref.shape[0]

    cast_out_ref[...] = cast_ref[...].astype(cast_out_ref.dtype)

    @pl.when(pl.program_id(0) % steps_per_seq == 0)
    def _():
        zbuf_ref[:, 0:SUBLANES, :] = jnp.zeros((CONV_WIDTH // LANES, SUBLANES, LANES), _F32)

    def project(r):
        r0 = r * MIX_SUB
        xb = x_ref[r0:r0 + MIX_SUB, :].astype(_BF16)
        q = _dot(xb, wq_ref[...]).astype(_BF16)
        cb, cc, ch = [_dot(xb, wc_ref[:, k * CONV_WIDTH:(k + 1) * CONV_WIDTH]) for k in range(3)]
        scores = []
        for h in range(XATTN_HEADS):
            hs = slice(h * XATTN_HEAD_DIM, (h + 1) * XATTN_HEAD_DIM)
            scores.append(lax.dot_general(q[:, hs], kv_ref[:, hs], (((1,), (1,)), ((), ())),
                                          preferred_element_type=_F32))
        gates = [_dot(xb, wg_ref[:, k * D_MODEL:(k + 1) * D_MODEL]) for k in range(N_BRANCH)]

        z = cc * ch
        for c in range(CONV_WIDTH // LANES):
            zbuf_ref[c, SUBLANES + r0:SUBLANES + r0 + MIX_SUB, :] = z[:, c * LANES:(c + 1) * LANES]

        def delayed(k):
            start = SUBLANES - k + r0
            return jnp.concatenate([zbuf_ref[c, start:start + MIX_SUB, :]
                                    for c in range(CONV_WIDTH // LANES)], axis=1)

        conv = cw_ref[0:1, :] * delayed(2) + cw_ref[1:2, :] * delayed(1) + cw_ref[2:3, :] * z
        ya_in = (cb * conv).astype(_BF16)

        probs, dens = [], []
        for h in range(XATTN_HEADS):
            s = scores[h] * (XATTN_HEAD_DIM ** -0.5)
            p = jnp.exp(s - jnp.max(s, axis=-1, keepdims=True))
            dens.append(jnp.sum(p, axis=-1, keepdims=True))
            probs.append(p.astype(_BF16))
        return ya_in, probs, dens, gates

    def mix(r, ya_in, probs, dens, gates):
        rows = slice(r * MIX_SUB, (r + 1) * MIX_SUB)

        def gate(k):
            return jax.nn.sigmoid(gates[k] + bg_ref[:, k * D_MODEL:(k + 1) * D_MODEL])

        glu = _dot(ys_ref[rows, :], wglu_ref[...])
        y_a = _dot(ya_in, wco_ref[...])
        heads = []
        for h in range(XATTN_HEADS):
            vs = slice(XATTN_WIDTH + h * XATTN_HEAD_DIM, XATTN_WIDTH + (h + 1) * XATTN_HEAD_DIM)
            heads.append((_dot(probs[h], kv_ref[:, vs]) / dens[h]).astype(_BF16))
        y_c = _dot(jnp.concatenate(heads, axis=1), wxo_ref[...])

        merged = gate(0) * y_a + gate(2) * y_c
        merged += gate(1) * (glu[:, :D_MODEL] * jax.nn.sigmoid(glu[:, D_MODEL:]))
        v = ALPHA * x_ref[rows, :] + _dot(merged.astype(_BF16), wo_ref[...])
        o_ref[rows, :] = _layer_norm(v, g_ref[...], b_ref[...])

    n_chains = tm_ // MIX_SUB
    pending = project(0)
    for r in range(n_chains):
        current = pending
        if r + 1 < n_chains:
            pending = project(r + 1)
        mix(r, *current)

    zbuf_ref[:, 0:SUBLANES, :] = zbuf_ref[:, tm_:tm_ + SUBLANES, :]


def _col_block_spec(width, start):
    assert start % width == 0
    return pl.BlockSpec((D_MODEL, width), lambda i: (0, start // width),
                        pipeline_mode=pl.Buffered(1))


def _layer_spec(param, layer):
    rest = param.shape[1:]
    if len(rest) == 1:
        return pl.BlockSpec((1,) + rest, lambda *_: (layer, 0), pipeline_mode=pl.Buffered(1))
    zeros = (0,) * len(rest)
    return pl.BlockSpec((None,) + rest, lambda *_: (layer,) + zeros,
                        pipeline_mode=pl.Buffered(1))


def _mixer_call(x2d, ys, kv, w_in_b, b_g, conv_w, w_co, w_glu, w_xo, w_o, ln_g, ln_b,
                layer, w_cast, seq_len):
    n = x2d.shape[0]
    steps = n // MIX_TOKENS
    steps_per_seq = seq_len // MIX_TOKENS
    conv_start = GATE_COLS
    q_start = GATE_COLS + 3 * CONV_WIDTH + SSM_WIDTH
    cast_rows, cast_cols = w_cast.shape[1:]
    cast_block = cast_rows // steps
    assert cast_block * steps == cast_rows and cast_block % (2 * SUBLANES) == 0
    operands = [w_in_b, b_g, w_in_b, conv_w, w_co, w_glu, w_in_b, w_xo, w_o, ln_g, ln_b, w_cast]
    specs = [_col_block_spec(GATE_COLS, 0), _layer_spec(b_g, layer),
             _col_block_spec(3 * CONV_WIDTH, conv_start), _layer_spec(conv_w, layer)]
    specs += [_const_spec(c.shape) for c in (w_co, w_glu)]
    specs += [_col_block_spec(XATTN_WIDTH, q_start)]
    specs += [_const_spec(c.shape) for c in (w_xo, w_o)]
    specs += [_layer_spec(ln_g, layer), _layer_spec(ln_b, layer),
              pl.BlockSpec((None, cast_block, cast_cols), lambda i: (layer, i, 0))]
    return pl.pallas_call(
        functools.partial(_mixer_kernel, steps_per_seq),
        out_shape=(jax.ShapeDtypeStruct((n, D_MODEL), _F32),
                   jax.ShapeDtypeStruct((cast_rows, cast_cols), _BF16)),
        grid=(steps,),
        in_specs=[pl.BlockSpec((MIX_TOKENS, D_MODEL), lambda i: (i, 0)),
                  pl.BlockSpec((MIX_TOKENS, SSM_WIDTH), lambda i: (i, 0)),
                  pl.BlockSpec((MEM_LEN, 2 * XATTN_WIDTH), lambda i: (i // steps_per_seq, 0))]
                 + specs,
        out_specs=(pl.BlockSpec((MIX_TOKENS, D_MODEL), lambda i: (i, 0)),
                   pl.BlockSpec((cast_block, cast_cols), lambda i: (i, 0))),
        scratch_shapes=[pltpu.VMEM((CONV_WIDTH // LANES, MIX_TOKENS + 2 * SUBLANES, LANES), _F32)],
        compiler_params=pltpu.CompilerParams(dimension_semantics=("arbitrary",),
                                             vmem_limit_bytes=VMEM_LIMIT),
        name="mixers",
    )(x2d, ys, kv, *operands)


def _mlp_kernel(x_ref, wu_ref, bu_ref, wd_ref, bd_ref, g_ref, b_ref, o_ref):
    n_chunks = D_FF // FF_CHUNK
    units = [(r, c) for r in range(MLP_TOKENS // MLP_SUB) for c in range(n_chunks)]
    xb, acc = {}, {}

    def up(r, c):
        if c == 0:
            xb[r] = x_ref[r * MLP_SUB:(r + 1) * MLP_SUB, :].astype(_BF16)
        return _dot(xb[r], wu_ref[:, c * FF_CHUNK:(c + 1) * FF_CHUNK])

    def down(r, c, pre):
        cols = slice(c * FF_CHUNK, (c + 1) * FF_CHUNK)
        h = jnp.maximum(pre + bu_ref[:, cols], 0.0)
        if c == 0:
            acc[r] = ALPHA * x_ref[r * MLP_SUB:(r + 1) * MLP_SUB, :] + bd_ref[...]
        acc[r] += _dot((h * h).astype(_BF16), wd_ref[cols, :])

    pre = up(*units[0])
    for k, (r, c) in enumerate(units):
        nxt = up(*units[k + 1]) if k + 1 < len(units) else None
        down(r, c, pre)
        if c == n_chunks - 1:
            o_ref[r * MLP_SUB:(r + 1) * MLP_SUB, :] = _layer_norm(acc.pop(r), g_ref[...],
                                                                  b_ref[...])
        pre = nxt


def _mlp_call(x1, w_up, b_up, w_down, b_down, ln_g, ln_b, layer):
    n = x1.shape[0]
    consts = [w_up, b_up, w_down, b_down, ln_g, ln_b]
    return pl.pallas_call(
        _mlp_kernel,
        out_shape=jax.ShapeDtypeStruct((n, D_MODEL), _F32),
        grid=(n // MLP_TOKENS,),
        in_specs=[pl.BlockSpec((MLP_TOKENS, D_MODEL), lambda i: (i, 0)),
                  _const_spec(w_up.shape), _layer_spec(b_up, layer),
                  _const_spec(w_down.shape), _layer_spec(b_down, layer),
                  _layer_spec(ln_g, layer), _layer_spec(ln_b, layer)],
        out_specs=pl.BlockSpec((MLP_TOKENS, D_MODEL), lambda i: (i, 0)),
        compiler_params=pltpu.CompilerParams(dimension_semantics=("arbitrary",),
                                             vmem_limit_bytes=VMEM_LIMIT),
        name="mlp",
    )(x1, *consts)


def kernel(x, mem, w_in, b_gate, conv_w, w_conv_out, ssm_lam_re, ssm_lam_im, ssm_log_dt,
           ssm_b_re, ssm_b_im, ssm_c_re, ssm_c_im, ssm_d, w_glu, w_kv, w_xattn_out, w_out,
           ln1_g, ln1_b, w_up, b_up, w_down, b_down, ln2_g, ln2_b):
    bsz, seq, d = x.shape
    assert d == D_MODEL and seq % (PACK * S5_ROWS) == 0 and seq % MIX_TOKENS == 0
    assert w_in.shape[0] == DEPTH == 1
    n = bsz * seq
    l = 0

    w_in_b = w_in[l].astype(_BF16)
    tables = _s5_prep(ssm_lam_re[l], ssm_lam_im[l], ssm_log_dt[l], ssm_b_re[l],
                      ssm_b_im[l], ssm_c_re[l], ssm_c_im[l])

    x2d = x.reshape(n, d)
    kv = _kv_call(mem.reshape(bsz * MEM_LEN, d), w_kv[l].astype(_BF16))
    ys, w_up_b = _s5_call(x2d, w_in_b, tables, ssm_d, l, w_up, seq)
    x1, w_down_b = _mixer_call(x2d, ys, kv, w_in_b, b_gate, conv_w,
                               w_conv_out[l].astype(_BF16), w_glu[l].astype(_BF16),
                               w_xattn_out[l].astype(_BF16), w_out[l].astype(_BF16),
                               ln1_g, ln1_b, l, w_down, seq)
    out = _mlp_call(x1, w_up_b, b_up, w_down_b, b_down, ln2_g, ln2_b, l)
    return out.reshape(bsz, seq, d)
```

```python
import functools

import jax
import jax.numpy as jnp
from jax import lax
from jax.experimental import pallas as pl
from jax.experimental.pallas import tpu as pltpu

D_MODEL = 1024
MEM_LEN = 256
N_BRANCH = 3
CONV_WIDTH = 512
CONV_TAPS = 3
SSM_WIDTH = 512
SSM_GROUP = 16
SSM_GROUPS = 32
SSM_STATE = 64
XATTN_HEADS = 4
XATTN_HEAD_DIM = 128
XATTN_WIDTH = 512
D_FF = 4096
GATE_COLS = N_BRANCH * D_MODEL
DEPTH = 1
ALPHA = (2.0 * DEPTH) ** 0.25
LN_EPS = 1e-5

LANES = 128
SUBLANES = 8
PACK = SUBLANES
GROUPS_PER_TILE = LANES // SSM_GROUP
N_LANE_TILES = SSM_WIDTH // LANES
STATE_PER_TILE = GROUPS_PER_TILE * SSM_STATE
N_STATE = SSM_GROUPS * SSM_STATE
PACKED_TILE = PACK * LANES

S5_ROWS = 128
TM_PANEL = 256
MIX_TOKENS = 512
MIX_SUB = 256
MLP_TOKENS = 1024
MLP_SUB = 256
FF_CHUNK = 1024
VMEM_LIMIT = 60 * 1024 * 1024

_BF16 = jnp.bfloat16
_F32 = jnp.float32
TAP_PRECISION = lax.Precision.HIGHEST


def _const_spec(shape):
    zeros = (0,) * len(shape)
    return pl.BlockSpec(shape, lambda *_: zeros, pipeline_mode=pl.Buffered(1))


def _dot(a, b):
    return jnp.dot(a, b, preferred_element_type=_F32)


def _layer_norm(v, g, b):
    mu = jnp.mean(v, axis=-1, keepdims=True)
    c = v - mu
    var = jnp.mean(c * c, axis=-1, keepdims=True)
    return c * lax.rsqrt(var + LN_EPS) * g + b


def _complex_powers(ar, ai, n):
    out = [(jnp.ones_like(ar), jnp.zeros_like(ai))]
    for _ in range(n):
        pr, pi = out[-1]
        out.append((pr * ar - pi * ai, pr * ai + pi * ar))
    return out


def _build_tables(rowp_ref, colp_ref, bt_ref, ct_ref, w1_ref, tm_ref, w3_ref, a8_ref):
    row_pow = _complex_powers(rowp_ref[0:1, :], rowp_ref[1:2, :], PACK)
    col_pow = _complex_powers(colp_ref[:, 0:1], colp_ref[:, 1:2], PACK)
    zr, zi = rowp_ref[2:3, :], rowp_ref[3:4, :]
    a8_ref[0], a8_ref[1] = row_pow[PACK]
    rows1 = lax.broadcasted_iota(jnp.int32, (LANES, STATE_PER_TILE), 0) // SSM_GROUP
    cols1 = lax.broadcasted_iota(jnp.int32, (LANES, STATE_PER_TILE), 1) // SSM_STATE
    same1 = rows1 == cols1
    rows3 = lax.broadcasted_iota(jnp.int32, (STATE_PER_TILE, LANES), 0) // SSM_STATE
    cols3 = lax.broadcasted_iota(jnp.int32, (STATE_PER_TILE, LANES), 1) // SSM_GROUP
    same3 = rows3 == cols3

    lane_reps = STATE_PER_TILE // LANES
    btr = jnp.where(same1, jnp.concatenate([bt_ref[0]] * lane_reps, axis=1), 0.0)
    bti = jnp.where(same1, jnp.concatenate([bt_ref[1]] * lane_reps, axis=1), 0.0)
    bbar = None
    for j in range(PACK):
        qr, qi = row_pow[PACK - 1 - j]
        pr, pi = qr * zr - qi * zi, qr * zi + qi * zr
        sre = pr * btr - pi * bti
        sim = pr * bti + pi * btr
        w1_ref[j * LANES:(j + 1) * LANES, :STATE_PER_TILE] = sre.astype(w1_ref.dtype)
        w1_ref[j * LANES:(j + 1) * LANES, STATE_PER_TILE:] = sim.astype(w1_ref.dtype)
        if j == PACK - 1:
            bbar = jnp.concatenate([sre, sim], axis=1)

    ctr = jnp.where(same3, ct_ref[0], 0.0)
    cti = jnp.where(same3, ct_ref[1], 0.0)
    nctr = -ctr
    tker = []
    for q in range(PACK + 1):
        pr, pi = col_pow[q]
        slab = jnp.concatenate([ctr * pr - cti * pi, nctr * pi - cti * pr], axis=0)
        if q >= 1:
            w3_ref[:, (q - 1) * LANES:q * LANES] = slab.astype(w3_ref.dtype)
        if q < PACK:
            tker.append(jnp.dot(bbar, slab, precision=TAP_PRECISION,
                                preferred_element_type=_F32).astype(tm_ref.dtype))

    zero = jnp.zeros((LANES, LANES), tm_ref.dtype)
    for j in range(PACK):
        for i in range(PACK):
            tm_ref[j * LANES:(j + 1) * LANES, i * LANES:(i + 1) * LANES] = (
                tker[i - j] if i >= j else zero)


def _s5_prep(lam_re, lam_im, log_dt, b_re, b_im, c_re, c_im):
    lr, li = lam_re.astype(_F32), lam_im.astype(_F32)
    dt = jnp.exp(log_dt.astype(_F32))[:, None]
    mag = jnp.exp(lr * dt)
    ar = mag * jnp.cos(li * dt)
    ai = mag * jnp.sin(li * dt)
    den = lr * lr + li * li
    nr, ni = ar - 1.0, ai
    kr = (nr * lr + ni * li) / den
    ki = (ni * lr - nr * li) / den
    t, g = N_LANE_TILES, GROUPS_PER_TILE

    rowp = jnp.stack([ar, ai, kr, ki]).reshape(4, t, STATE_PER_TILE).transpose(1, 0, 2)
    colp = rowp.transpose(0, 2, 1)
    b = jnp.stack([b_re, b_im]).astype(_F32).reshape(2, t, g, SSM_STATE, SSM_GROUP)
    reps = LANES // SSM_STATE
    bt = jnp.broadcast_to(b.transpose(1, 0, 2, 4, 3)[:, :, :, :, None, :],
                          (t, 2, g, SSM_GROUP, reps, SSM_STATE)).reshape(t, 2, LANES, LANES)
    c = jnp.stack([c_re, c_im]).astype(_F32).reshape(2, t, g, SSM_GROUP, SSM_STATE)
    ct = jnp.broadcast_to(c.transpose(1, 0, 2, 4, 3)[:, :, :, :, None, :],
                          (t, 2, g, SSM_STATE, g, SSM_GROUP)).reshape(t, 2, STATE_PER_TILE, LANES)
    return rowp, colp, bt, ct


def _s5_kernel(steps_per_seq, x_ref, wu_ref, rowp_ref, colp_ref, bt_ref, ct_ref, d_ref, cast_ref,
               o_ref, cast_out_ref, w1_ref, tm_ref, w3_ref, a8t_ref, a8_ref, u_ref, xp_ref, yi_ref, y_ref,
               zr_ref, zi_ref, sr_ref, si_ref, carry_ref):
    rows = x_ref.shape[0] // PACK

    cast_out_ref[...] = cast_ref[...].astype(cast_out_ref.dtype)

    @pl.when(pl.program_id(0) == 0)
    def _():
        def build(t, c):
            _build_tables(rowp_ref.at[t], colp_ref.at[t], bt_ref.at[t], ct_ref.at[t],
                          w1_ref.at[t], tm_ref.at[t], w3_ref.at[t], a8t_ref.at[t])
            return c
        lax.fori_loop(0, N_LANE_TILES, build, 0)
        for t in range(N_LANE_TILES):
            a8_ref[:, :, t * STATE_PER_TILE:(t + 1) * STATE_PER_TILE] = a8t_ref[t]

    @pl.when(pl.program_id(0) % steps_per_seq == 0)
    def _():
        carry_ref[...] = jnp.zeros_like(carry_ref)

    u = _dot(x_ref[...].astype(_BF16), wu_ref[...])
    for t in range(N_LANE_TILES):
        u_ref[t] = u[:, t * LANES:(t + 1) * LANES]

    for t in range(N_LANE_TILES):
        xp_ref[t] = jnp.concatenate(
            [u_ref[t, pl.ds(j, rows, stride=PACK), :] for j in range(PACK)],
            axis=1).astype(_BF16)
        z = _dot(xp_ref[t], w1_ref[t])
        zr_ref[:, t * STATE_PER_TILE:(t + 1) * STATE_PER_TILE] = z[:, :STATE_PER_TILE]
        zi_ref[:, t * STATE_PER_TILE:(t + 1) * STATE_PER_TILE] = z[:, STATE_PER_TILE:]

    a_r = a8_ref[0]
    a_i = a8_ref[1]
    s_r = carry_ref[0]
    s_i = carry_ref[1]
    seg = rows // N_LANE_TILES
    for t in range(N_LANE_TILES):
        for c in range(PACKED_TILE // TM_PANEL):
            hi = (c + 1) * TM_PANEL
            yi_ref[t, :, c * TM_PANEL:hi] = _dot(xp_ref[t, :, :hi],
                                                 tm_ref[t, :hi, c * TM_PANEL:hi])
        for r in range(t * seg, (t + 1) * seg):
            sr_ref[r:r + 1, :] = s_r
            si_ref[r:r + 1, :] = s_i
            z_r = zr_ref[r:r + 1, :]
            z_i = zi_ref[r:r + 1, :]
            s_r, s_i = a_r * s_r - a_i * s_i + z_r, a_r * s_i + a_i * s_r + z_i
    carry_ref[0] = s_r
    carry_ref[1] = s_i

    for t in range(N_LANE_TILES):
        sl = slice(t * STATE_PER_TILE, (t + 1) * STATE_PER_TILE)
        sprev = jnp.concatenate([sr_ref[:, sl], si_ref[:, sl]], axis=1).astype(_BF16)
        y = yi_ref[t] + _dot(sprev, w3_ref[t])
        for i in range(PACK):
            y_ref[t, pl.ds(i, rows, stride=PACK), :] = y[:, i * LANES:(i + 1) * LANES]
        cols = slice(t * LANES, (t + 1) * LANES)
        o_ref[:, cols] = jax.nn.gelu(y_ref[t] + d_ref[:, cols] * u_ref[t]).astype(o_ref.dtype)


def _s5_call(x2d, w_in_b, tables, d_skip, layer, w_cast, seq_len):
    rowp, colp, bt, ct = tables
    n = x2d.shape[0]
    tokens = S5_ROWS * PACK
    steps = n // tokens
    steps_per_seq = seq_len // tokens
    u_block = (GATE_COLS + 3 * CONV_WIDTH) // SSM_WIDTH
    cast_rows, cast_cols = w_cast.shape[1:]
    cast_block = cast_cols // steps
    assert cast_block * steps == cast_cols and cast_block % LANES == 0
    table = pltpu.VMEM((N_LANE_TILES, PACKED_TILE, PACKED_TILE), _BF16)
    slab = pltpu.VMEM((N_LANE_TILES, tokens, LANES), _F32)
    state = pltpu.VMEM((S5_ROWS, N_STATE), _F32)
    return pl.pallas_call(
        functools.partial(_s5_kernel, steps_per_seq),
        out_shape=(jax.ShapeDtypeStruct((n, SSM_WIDTH), _BF16),
                   jax.ShapeDtypeStruct((cast_rows, cast_cols), _BF16)),
        grid=(steps,),
        in_specs=[pl.BlockSpec((tokens, D_MODEL), lambda i: (i, 0)),
                  pl.BlockSpec((D_MODEL, SSM_WIDTH), lambda i: (0, u_block),
                               pipeline_mode=pl.Buffered(1))]
                 + [_const_spec(c.shape) for c in (rowp, colp, bt, ct)]
                 + [_layer_spec(d_skip, layer),
                    pl.BlockSpec((None, cast_rows, cast_block), lambda i: (layer, 0, i))],
        out_specs=(pl.BlockSpec((tokens, SSM_WIDTH), lambda i: (i, 0)),
                   pl.BlockSpec((cast_rows, cast_block), lambda i: (0, i))),
        scratch_shapes=[table, table, table,
                        pltpu.VMEM((N_LANE_TILES, 2, 1, STATE_PER_TILE), _F32),
                        pltpu.VMEM((2, 1, N_STATE), _F32), slab,
                        pltpu.VMEM((N_LANE_TILES, S5_ROWS, PACKED_TILE), _BF16),
                        pltpu.VMEM((N_LANE_TILES, S5_ROWS, PACKED_TILE), _F32),
                        slab, state, state, state, state,
                        pltpu.VMEM((2, 1, N_STATE), _F32)],
        compiler_params=pltpu.CompilerParams(dimension_semantics=("arbitrary",),
                                             vmem_limit_bytes=VMEM_LIMIT),
        name="s5",
    )(x2d, w_in_b, rowp, colp, bt, ct, d_skip, w_cast)


def _mixer_kernel(steps_per_seq, x_ref, ys_ref, mem_ref, wg_ref, bg_ref, wc_ref, cw_ref,
                  wco32_ref, wglu32_ref, wq_ref, wxo32_ref, wo32_ref, wkv32_ref, g_ref, b_ref,
                  cast_ref, o_ref, cast_out_ref,
                  wco_ref, wglu_ref, wxo_ref, wo_ref, wkv_ref, kv_ref, zbuf_ref):
    tm_ = x_ref.shape[0]

    cast_out_ref[...] = cast_ref[...].astype(cast_out_ref.dtype)

    @pl.when(pl.program_id(0) == 0)
    def _():
        for src, dst in ((wco32_ref, wco_ref), (wglu32_ref, wglu_ref), (wxo32_ref, wxo_ref),
                         (wo32_ref, wo_ref), (wkv32_ref, wkv_ref)):
            dst[...] = src[...].astype(dst.dtype)

    @pl.when(pl.program_id(0) % steps_per_seq == 0)
    def _():
        kv_ref[...] = _dot(mem_ref[...].astype(_BF16), wkv_ref[...]).astype(kv_ref.dtype)
        zbuf_ref[:, 0:SUBLANES, :] = jnp.zeros((CONV_WIDTH // LANES, SUBLANES, LANES), _F32)

    def project(r):
        r0 = r * MIX_SUB
        xb = x_ref[r0:r0 + MIX_SUB, :].astype(_BF16)
        q = _dot(xb, wq_ref[...]).astype(_BF16)
        cb, cc, ch = [_dot(xb, wc_ref[:, k * CONV_WIDTH:(k + 1) * CONV_WIDTH]) for k in range(3)]
        scores = []
        for h in range(XATTN_HEADS):
            hs = slice(h * XATTN_HEAD_DIM, (h + 1) * XATTN_HEAD_DIM)
            scores.append(lax.dot_general(q[:, hs], kv_ref[:, hs], (((1,), (1,)), ((), ())),
                                          preferred_element_type=_F32))
        gates = [_dot(xb, wg_ref[:, k * D_MODEL:(k + 1) * D_MODEL]) for k in range(N_BRANCH)]

        z = cc * ch
        for c in range(CONV_WIDTH // LANES):
            zbuf_ref[c, SUBLANES + r0:SUBLANES + r0 + MIX_SUB, :] = z[:, c * LANES:(c + 1) * LANES]

        def delayed(k):
            start = SUBLANES - k + r0
            return jnp.concatenate([zbuf_ref[c, start:start + MIX_SUB, :]
                                    for c in range(CONV_WIDTH // LANES)], axis=1)

        conv = cw_ref[0:1, :] * delayed(2) + cw_ref[1:2, :] * delayed(1) + cw_ref[2:3, :] * z
        ya_in = (cb * conv).astype(_BF16)

        probs, dens = [], []
        for h in range(XATTN_HEADS):
            s = scores[h] * (XATTN_HEAD_DIM ** -0.5)
            p = jnp.exp(s - jnp.max(s, axis=-1, keepdims=True))
            dens.append(jnp.sum(p, axis=-1, keepdims=True))
            probs.append(p.astype(_BF16))
        return ya_in, probs, dens, gates

    def mix(r, ya_in, probs, dens, gates):
        rows = slice(r * MIX_SUB, (r + 1) * MIX_SUB)

        def gate(k):
            return jax.nn.sigmoid(gates[k] + bg_ref[:, k * D_MODEL:(k + 1) * D_MODEL])

        glu = _dot(ys_ref[rows, :], wglu_ref[...])
        y_a = _dot(ya_in, wco_ref[...])
        heads = []
        for h in range(XATTN_HEADS):
            vs = slice(XATTN_WIDTH + h * XATTN_HEAD_DIM, XATTN_WIDTH + (h + 1) * XATTN_HEAD_DIM)
            heads.append((_dot(probs[h], kv_ref[:, vs]) / dens[h]).astype(_BF16))
        y_c = _dot(jnp.concatenate(heads, axis=1), wxo_ref[...])

        merged = gate(0) * y_a + gate(2) * y_c
        merged += gate(1) * (glu[:, :D_MODEL] * jax.nn.sigmoid(glu[:, D_MODEL:]))
        v = ALPHA * x_ref[rows, :] + _dot(merged.astype(_BF16), wo_ref[...])
        o_ref[rows, :] = _layer_norm(v, g_ref[...], b_ref[...])

    n_chains = tm_ // MIX_SUB
    pending = project(0)
    for r in range(n_chains):
        current = pending
        if r + 1 < n_chains:
            pending = project(r + 1)
        mix(r, *current)

    zbuf_ref[:, 0:SUBLANES, :] = zbuf_ref[:, tm_:tm_ + SUBLANES, :]


def _col_block_spec(width, start):
    assert start % width == 0
    return pl.BlockSpec((D_MODEL, width), lambda i: (0, start // width),
                        pipeline_mode=pl.Buffered(1))


def _layer_spec(param, layer):
    rest = param.shape[1:]
    if len(rest) == 1:
        return pl.BlockSpec((1,) + rest, lambda *_: (layer, 0), pipeline_mode=pl.Buffered(1))
    zeros = (0,) * len(rest)
    return pl.BlockSpec((None,) + rest, lambda *_: (layer,) + zeros,
                        pipeline_mode=pl.Buffered(1))


def _mixer_call(x2d, ys, mem2d, w_in_b, b_g, conv_w, w_co, w_glu, w_xo, w_o, w_kv, ln_g, ln_b,
                layer, w_cast, seq_len):
    n = x2d.shape[0]
    steps = n // MIX_TOKENS
    steps_per_seq = seq_len // MIX_TOKENS
    conv_start = GATE_COLS
    q_start = GATE_COLS + 3 * CONV_WIDTH + SSM_WIDTH
    cast_rows, cast_cols = w_cast.shape[1:]
    cast_block = cast_rows // steps
    assert cast_block * steps == cast_rows and cast_block % (2 * SUBLANES) == 0
    small = (w_co, w_glu, w_xo, w_o, w_kv)
    operands = [w_in_b, b_g, w_in_b, conv_w, w_co, w_glu, w_in_b, w_xo, w_o, w_kv,
                ln_g, ln_b, w_cast]
    specs = [_col_block_spec(GATE_COLS, 0), _layer_spec(b_g, layer),
             _col_block_spec(3 * CONV_WIDTH, conv_start), _layer_spec(conv_w, layer),
             _layer_spec(w_co, layer), _layer_spec(w_glu, layer),
             _col_block_spec(XATTN_WIDTH, q_start),
             _layer_spec(w_xo, layer), _layer_spec(w_o, layer), _layer_spec(w_kv, layer),
             _layer_spec(ln_g, layer), _layer_spec(ln_b, layer),
             pl.BlockSpec((None, cast_block, cast_cols), lambda i: (layer, i, 0))]
    return pl.pallas_call(
        functools.partial(_mixer_kernel, steps_per_seq),
        out_shape=(jax.ShapeDtypeStruct((n, D_MODEL), _F32),
                   jax.ShapeDtypeStruct((cast_rows, cast_cols), _BF16)),
        grid=(steps,),
        in_specs=[pl.BlockSpec((MIX_TOKENS, D_MODEL), lambda i: (i, 0)),
                  pl.BlockSpec((MIX_TOKENS, SSM_WIDTH), lambda i: (i, 0)),
                  pl.BlockSpec((MEM_LEN, D_MODEL), lambda i: (i // steps_per_seq, 0))]
                 + specs,
        out_specs=(pl.BlockSpec((MIX_TOKENS, D_MODEL), lambda i: (i, 0)),
                   pl.BlockSpec((cast_block, cast_cols), lambda i: (i, 0))),
        scratch_shapes=[pltpu.VMEM(w.shape[1:], _BF16) for w in small]
                       + [pltpu.VMEM((MEM_LEN, 2 * XATTN_WIDTH), _BF16),
                          pltpu.VMEM((CONV_WIDTH // LANES, MIX_TOKENS + 2 * SUBLANES, LANES),
                                     _F32)],
        compiler_params=pltpu.CompilerParams(dimension_semantics=("arbitrary",),
                                             vmem_limit_bytes=VMEM_LIMIT),
        name="mixers",
    )(x2d, ys, mem2d, *operands)


def _mlp_kernel(x_ref, wu_ref, bu_ref, wd_ref, bd_ref, g_ref, b_ref, o_ref):
    n_chunks = D_FF // FF_CHUNK
    units = [(r, c) for r in range(MLP_TOKENS // MLP_SUB) for c in range(n_chunks)]
    xb, acc = {}, {}

    def up(r, c):
        if c == 0:
            xb[r] = x_ref[r * MLP_SUB:(r + 1) * MLP_SUB, :].astype(_BF16)
        return _dot(xb[r], wu_ref[:, c * FF_CHUNK:(c + 1) * FF_CHUNK])

    def down(r, c, pre):
        cols = slice(c * FF_CHUNK, (c + 1) * FF_CHUNK)
        h = jnp.maximum(pre + bu_ref[:, cols], 0.0)
        if c == 0:
            acc[r] = ALPHA * x_ref[r * MLP_SUB:(r + 1) * MLP_SUB, :] + bd_ref[...]
        acc[r] += _dot((h * h).astype(_BF16), wd_ref[cols, :])

    pre = up(*units[0])
    for k, (r, c) in enumerate(units):
        nxt = up(*units[k + 1]) if k + 1 < len(units) else None
        down(r, c, pre)
        if c == n_chunks - 1:
            o_ref[r * MLP_SUB:(r + 1) * MLP_SUB, :] = _layer_norm(acc.pop(r), g_ref[...],
                                                                  b_ref[...])
        pre = nxt


def _mlp_call(x1, w_up, b_up, w_down, b_down, ln_g, ln_b, layer):
    n = x1.shape[0]
    consts = [w_up, b_up, w_down, b_down, ln_g, ln_b]
    return pl.pallas_call(
        _mlp_kernel,
        out_shape=jax.ShapeDtypeStruct((n, D_MODEL), _F32),
        grid=(n // MLP_TOKENS,),
        in_specs=[pl.BlockSpec((MLP_TOKENS, D_MODEL), lambda i: (i, 0)),
                  _const_spec(w_up.shape), _layer_spec(b_up, layer),
                  _const_spec(w_down.shape), _layer_spec(b_down, layer),
                  _layer_spec(ln_g, layer), _layer_spec(ln_b, layer)],
        out_specs=pl.BlockSpec((MLP_TOKENS, D_MODEL), lambda i: (i, 0)),
        compiler_params=pltpu.CompilerParams(dimension_semantics=("arbitrary",),
                                             vmem_limit_bytes=VMEM_LIMIT),
        name="mlp",
    )(x1, *consts)


def kernel(x, mem, w_in, b_gate, conv_w, w_conv_out, ssm_lam_re, ssm_lam_im, ssm_log_dt,
           ssm_b_re, ssm_b_im, ssm_c_re, ssm_c_im, ssm_d, w_glu, w_kv, w_xattn_out, w_out,
           ln1_g, ln1_b, w_up, b_up, w_down, b_down, ln2_g, ln2_b):
    bsz, seq, d = x.shape
    assert d == D_MODEL and seq % (PACK * S5_ROWS) == 0 and seq % MIX_TOKENS == 0
    assert w_in.shape[0] == DEPTH == 1
    n = bsz * seq
    l = 0

    w_in_b = w_in[l].astype(_BF16)
    tables = _s5_prep(ssm_lam_re[l], ssm_lam_im[l], ssm_log_dt[l], ssm_b_re[l],
                      ssm_b_im[l], ssm_c_re[l], ssm_c_im[l])

    x2d = x.reshape(n, d)
    ys, w_up_b = _s5_call(x2d, w_in_b, tables, ssm_d, l, w_up, seq)
    x1, w_down_b = _mixer_call(x2d, ys, mem.reshape(bsz * MEM_LEN, d), w_in_b, b_gate, conv_w,
                               w_conv_out, w_glu, w_xattn_out, w_out, w_kv,
                               ln1_g, ln1_b, l, w_down, seq)
    out = _mlp_call(x1, w_up_b, b_up, w_down_b, b_down, ln2_g, ln2_b, l)
    return out.reshape(bsz, seq, d)
```

```python
import functools

import jax
import jax.numpy as jnp
from jax import lax
from jax.experimental import pallas as pl
from jax.experimental.pallas import tpu as pltpu

D_MODEL = 1024
MEM_LEN = 256
N_BRANCH = 3
CONV_WIDTH = 512
CONV_TAPS = 3
SSM_WIDTH = 512
SSM_GROUP = 16
SSM_GROUPS = 32
SSM_STATE = 64
XATTN_HEADS = 4
XATTN_HEAD_DIM = 128
XATTN_WIDTH = 512
D_FF = 4096
GATE_COLS = N_BRANCH * D_MODEL
DEPTH = 1
ALPHA = (2.0 * DEPTH) ** 0.25
LN_EPS = 1e-5

LANES = 128
SUBLANES = 8
PACK = SUBLANES
GROUPS_PER_TILE = LANES // SSM_GROUP
N_LANE_TILES = SSM_WIDTH // LANES
STATE_PER_TILE = GROUPS_PER_TILE * SSM_STATE
N_STATE = SSM_GROUPS * SSM_STATE
PACKED_TILE = PACK * LANES

S5_ROWS = 128
TM_PANEL = 256
MIX_TOKENS = 512
MIX_SUB = 256
STAGE_COLS = 512
MLP_TOKENS = 1024
MLP_SUB = 256
FF_CHUNK = 1024
VMEM_LIMIT = 60 * 1024 * 1024

_BF16 = jnp.bfloat16
_F32 = jnp.float32
TAP_PRECISION = lax.Precision.HIGHEST


def _const_spec(shape):
    zeros = (0,) * len(shape)
    return pl.BlockSpec(shape, lambda *_: zeros, pipeline_mode=pl.Buffered(1))


def _dot(a, b):
    return jnp.dot(a, b, preferred_element_type=_F32)


def _layer_norm(v, g, b):
    mu = jnp.mean(v, axis=-1, keepdims=True)
    c = v - mu
    var = jnp.mean(c * c, axis=-1, keepdims=True)
    return c * lax.rsqrt(var + LN_EPS) * g + b


def _complex_powers(ar, ai, n):
    out = [(jnp.ones_like(ar), jnp.zeros_like(ai))]
    for _ in range(n):
        pr, pi = out[-1]
        out.append((pr * ar - pi * ai, pr * ai + pi * ar))
    return out


def _build_tables(rowp_ref, colp_ref, bt_ref, ct_ref, w1_ref, tm_ref, w3_ref, a8_ref):
    row_pow = _complex_powers(rowp_ref[0:1, :], rowp_ref[1:2, :], PACK)
    col_pow = _complex_powers(colp_ref[:, 0:1], colp_ref[:, 1:2], PACK)
    zr, zi = rowp_ref[2:3, :], rowp_ref[3:4, :]
    a8_ref[0], a8_ref[1] = row_pow[PACK]
    rows1 = lax.broadcasted_iota(jnp.int32, (LANES, STATE_PER_TILE), 0) // SSM_GROUP
    cols1 = lax.broadcasted_iota(jnp.int32, (LANES, STATE_PER_TILE), 1) // SSM_STATE
    same1 = rows1 == cols1
    rows3 = lax.broadcasted_iota(jnp.int32, (STATE_PER_TILE, LANES), 0) // SSM_STATE
    cols3 = lax.broadcasted_iota(jnp.int32, (STATE_PER_TILE, LANES), 1) // SSM_GROUP
    same3 = rows3 == cols3

    lane_reps = STATE_PER_TILE // LANES
    btr = jnp.where(same1, jnp.concatenate([bt_ref[0]] * lane_reps, axis=1), 0.0)
    bti = jnp.where(same1, jnp.concatenate([bt_ref[1]] * lane_reps, axis=1), 0.0)
    bbar = None
    for j in range(PACK):
        qr, qi = row_pow[PACK - 1 - j]
        pr, pi = qr * zr - qi * zi, qr * zi + qi * zr
        sre = pr * btr - pi * bti
        sim = pr * bti + pi * btr
        w1_ref[j * LANES:(j + 1) * LANES, :STATE_PER_TILE] = sre.astype(w1_ref.dtype)
        w1_ref[j * LANES:(j + 1) * LANES, STATE_PER_TILE:] = sim.astype(w1_ref.dtype)
        if j == PACK - 1:
            bbar = jnp.concatenate([sre, sim], axis=1)

    ctr = jnp.where(same3, ct_ref[0], 0.0)
    cti = jnp.where(same3, ct_ref[1], 0.0)
    nctr = -ctr
    tker = []
    for q in range(PACK + 1):
        pr, pi = col_pow[q]
        slab = jnp.concatenate([ctr * pr - cti * pi, nctr * pi - cti * pr], axis=0)
        if q >= 1:
            w3_ref[:, (q - 1) * LANES:q * LANES] = slab.astype(w3_ref.dtype)
        if q < PACK:
            tker.append(jnp.dot(bbar, slab, precision=TAP_PRECISION,
                                preferred_element_type=_F32).astype(tm_ref.dtype))

    zero = jnp.zeros((LANES, LANES), tm_ref.dtype)
    for j in range(PACK):
        for i in range(PACK):
            tm_ref[j * LANES:(j + 1) * LANES, i * LANES:(i + 1) * LANES] = (
                tker[i - j] if i >= j else zero)


def _s5_prep(lam_re, lam_im, log_dt, b_re, b_im, c_re, c_im):
    lr, li = lam_re.astype(_F32), lam_im.astype(_F32)
    dt = jnp.exp(log_dt.astype(_F32))[:, None]
    mag = jnp.exp(lr * dt)
    ar = mag * jnp.cos(li * dt)
    ai = mag * jnp.sin(li * dt)
    den = lr * lr + li * li
    nr, ni = ar - 1.0, ai
    kr = (nr * lr + ni * li) / den
    ki = (ni * lr - nr * li) / den
    t, g = N_LANE_TILES, GROUPS_PER_TILE

    rowp = jnp.stack([ar, ai, kr, ki]).reshape(4, t, STATE_PER_TILE).transpose(1, 0, 2)
    colp = rowp.transpose(0, 2, 1)
    b = jnp.stack([b_re, b_im]).astype(_F32).reshape(2, t, g, SSM_STATE, SSM_GROUP)
    reps = LANES // SSM_STATE
    bt = jnp.broadcast_to(b.transpose(1, 0, 2, 4, 3)[:, :, :, :, None, :],
                          (t, 2, g, SSM_GROUP, reps, SSM_STATE)).reshape(t, 2, LANES, LANES)
    c = jnp.stack([c_re, c_im]).astype(_F32).reshape(2, t, g, SSM_GROUP, SSM_STATE)
    ct = jnp.broadcast_to(c.transpose(1, 0, 2, 4, 3)[:, :, :, :, None, :],
                          (t, 2, g, SSM_STATE, g, SSM_GROUP)).reshape(t, 2, STATE_PER_TILE, LANES)
    return rowp, colp, bt, ct


def _s5_kernel(steps_per_seq, x_ref, wu32_ref, rowp_ref, colp_ref, bt_ref, ct_ref, d_ref, cast_ref,
               o_ref, cast_out_ref, wu_ref, w1_ref, tm_ref, w3_ref, a8t_ref, a8_ref,
               u_ref, xp_ref, yi_ref, y_ref, zr_ref, zi_ref, sr_ref, si_ref, carry_ref):
    rows = x_ref.shape[0] // PACK

    cast_out_ref[...] = cast_ref[...].astype(cast_out_ref.dtype)

    @pl.when(pl.program_id(0) == 0)
    def _():
        wu_ref[...] = wu32_ref[...].astype(wu_ref.dtype)

        def build(t, c):
            _build_tables(rowp_ref.at[t], colp_ref.at[t], bt_ref.at[t], ct_ref.at[t],
                          w1_ref.at[t], tm_ref.at[t], w3_ref.at[t], a8t_ref.at[t])
            return c
        lax.fori_loop(0, N_LANE_TILES, build, 0)
        for t in range(N_LANE_TILES):
            a8_ref[:, :, t * STATE_PER_TILE:(t + 1) * STATE_PER_TILE] = a8t_ref[t]

    @pl.when(pl.program_id(0) % steps_per_seq == 0)
    def _():
        carry_ref[...] = jnp.zeros_like(carry_ref)

    u = _dot(x_ref[...].astype(_BF16), wu_ref[...])
    for t in range(N_LANE_TILES):
        u_ref[t] = u[:, t * LANES:(t + 1) * LANES]

    for t in range(N_LANE_TILES):
        xp_ref[t] = jnp.concatenate(
            [u_ref[t, pl.ds(j, rows, stride=PACK), :] for j in range(PACK)],
            axis=1).astype(_BF16)
        z = _dot(xp_ref[t], w1_ref[t])
        zr_ref[:, t * STATE_PER_TILE:(t + 1) * STATE_PER_TILE] = z[:, :STATE_PER_TILE]
        zi_ref[:, t * STATE_PER_TILE:(t + 1) * STATE_PER_TILE] = z[:, STATE_PER_TILE:]

    a_r = a8_ref[0]
    a_i = a8_ref[1]
    s_r = carry_ref[0]
    s_i = carry_ref[1]
    seg = rows // N_LANE_TILES
    for t in range(N_LANE_TILES):
        for c in range(PACKED_TILE // TM_PANEL):
            hi = (c + 1) * TM_PANEL
            yi_ref[t, :, c * TM_PANEL:hi] = _dot(xp_ref[t, :, :hi],
                                                 tm_ref[t, :hi, c * TM_PANEL:hi])
        for r in range(t * seg, (t + 1) * seg):
            sr_ref[r:r + 1, :] = s_r
            si_ref[r:r + 1, :] = s_i
            z_r = zr_ref[r:r + 1, :]
            z_i = zi_ref[r:r + 1, :]
            s_r, s_i = a_r * s_r - a_i * s_i + z_r, a_r * s_i + a_i * s_r + z_i
    carry_ref[0] = s_r
    carry_ref[1] = s_i

    for t in range(N_LANE_TILES):
        sl = slice(t * STATE_PER_TILE, (t + 1) * STATE_PER_TILE)
        sprev = jnp.concatenate([sr_ref[:, sl], si_ref[:, sl]], axis=1).astype(_BF16)
        y = yi_ref[t] + _dot(sprev, w3_ref[t])
        for i in range(PACK):
            y_ref[t, pl.ds(i, rows, stride=PACK), :] = y[:, i * LANES:(i + 1) * LANES]
        cols = slice(t * LANES, (t + 1) * LANES)
        o_ref[:, cols] = jax.nn.gelu(y_ref[t] + d_ref[:, cols] * u_ref[t]).astype(o_ref.dtype)


def _s5_call(x2d, w_in, tables, d_skip, layer, w_cast, seq_len):
    rowp, colp, bt, ct = tables
    n = x2d.shape[0]
    tokens = S5_ROWS * PACK
    steps = n // tokens
    steps_per_seq = seq_len // tokens
    u_block = (GATE_COLS + 3 * CONV_WIDTH) // SSM_WIDTH
    cast_rows, cast_cols = w_cast.shape[1:]
    cast_block = cast_cols // steps
    assert cast_block * steps == cast_cols and cast_block % LANES == 0
    table = pltpu.VMEM((N_LANE_TILES, PACKED_TILE, PACKED_TILE), _BF16)
    slab = pltpu.VMEM((N_LANE_TILES, tokens, LANES), _F32)
    state = pltpu.VMEM((S5_ROWS, N_STATE), _F32)
    return pl.pallas_call(
        functools.partial(_s5_kernel, steps_per_seq),
        out_shape=(jax.ShapeDtypeStruct((n, SSM_WIDTH), _BF16),
                   jax.ShapeDtypeStruct((cast_rows, cast_cols), _BF16)),
        grid=(steps,),
        in_specs=[pl.BlockSpec((tokens, D_MODEL), lambda i: (i, 0)),
                  pl.BlockSpec((None, D_MODEL, SSM_WIDTH), lambda i: (layer, 0, u_block),
                               pipeline_mode=pl.Buffered(1))]
                 + [_const_spec(c.shape) for c in (rowp, colp, bt, ct)]
                 + [_layer_spec(d_skip, layer),
                    pl.BlockSpec((None, cast_rows, cast_block), lambda i: (layer, 0, i))],
        out_specs=(pl.BlockSpec((tokens, SSM_WIDTH), lambda i: (i, 0)),
                   pl.BlockSpec((cast_rows, cast_block), lambda i: (0, i))),
        scratch_shapes=[pltpu.VMEM((D_MODEL, SSM_WIDTH), _BF16), table, table, table,
                        pltpu.VMEM((N_LANE_TILES, 2, 1, STATE_PER_TILE), _F32),
                        pltpu.VMEM((2, 1, N_STATE), _F32), slab,
                        pltpu.VMEM((N_LANE_TILES, S5_ROWS, PACKED_TILE), _BF16),
                        pltpu.VMEM((N_LANE_TILES, S5_ROWS, PACKED_TILE), _F32),
                        slab, state, state, state, state,
                        pltpu.VMEM((2, 1, N_STATE), _F32)],
        compiler_params=pltpu.CompilerParams(dimension_semantics=("arbitrary",),
                                             vmem_limit_bytes=VMEM_LIMIT),
        name="s5",
    )(x2d, w_in, rowp, colp, bt, ct, d_skip, w_cast)


def _mixer_kernel(steps_per_seq, layer, x_ref, ys_ref, mem_ref, win_hbm, bg_ref, cw_ref,
                  wco32_ref, wglu32_ref, wxo32_ref, wo32_ref, wkv32_ref, g_ref, b_ref,
                  cast_ref, o_ref, cast_out_ref,
                  wg_ref, wc_ref, wq_ref, wco_ref, wglu_ref, wxo_ref, wo_ref, wkv_ref, kv_ref,
                  stage_ref, stage_sem, zbuf_ref):
    tm_ = x_ref.shape[0]

    cast_out_ref[...] = cast_ref[...].astype(cast_out_ref.dtype)

    @pl.when(pl.program_id(0) == 0)
    def _():
        blocks = [(wg_ref, 0), (wc_ref, GATE_COLS),
                  (wq_ref, GATE_COLS + 3 * CONV_WIDTH + SSM_WIDTH)]
        pieces = [(dst, c, src0 + c) for dst, src0 in blocks
                  for c in range(0, dst.shape[1], STAGE_COLS)]

        def copy(k):
            return pltpu.make_async_copy(
                win_hbm.at[layer, :, pl.ds(pieces[k][2], STAGE_COLS)],
                stage_ref.at[k % 2], stage_sem.at[k % 2])

        copy(0).start()
        for k, (dst, c, _) in enumerate(pieces):
            if k + 1 < len(pieces):
                copy(k + 1).start()
            copy(k).wait()
            dst[:, c:c + STAGE_COLS] = stage_ref[k % 2].astype(dst.dtype)

        for src, dst in ((wco32_ref, wco_ref), (wglu32_ref, wglu_ref), (wxo32_ref, wxo_ref),
                         (wo32_ref, wo_ref), (wkv32_ref, wkv_ref)):
            dst[...] = src[...].astype(dst.dtype)

    @pl.when(pl.program_id(0) % steps_per_seq == 0)
    def _():
        kv_ref[...] = _dot(mem_ref[...].astype(_BF16), wkv_ref[...]).astype(kv_ref.dtype)
        zbuf_ref[:, 0:SUBLANES, :] = jnp.zeros((CONV_WIDTH // LANES, SUBLANES, LANES), _F32)

    def project(r):
        r0 = r * MIX_SUB
        xb = x_ref[r0:r0 + MIX_SUB, :].astype(_BF16)
        q = _dot(xb, wq_ref[...]).astype(_BF16)
        cb, cc, ch = [_dot(xb, wc_ref[:, k * CONV_WIDTH:(k + 1) * CONV_WIDTH]) for k in range(3)]
        scores = []
        for h in range(XATTN_HEADS):
            hs = slice(h * XATTN_HEAD_DIM, (h + 1) * XATTN_HEAD_DIM)
            scores.append(lax.dot_general(q[:, hs], kv_ref[:, hs], (((1,), (1,)), ((), ())),
                                          preferred_element_type=_F32))
        gates = [_dot(xb, wg_ref[:, k * D_MODEL:(k + 1) * D_MODEL]) for k in range(N_BRANCH)]

        z = cc * ch
        for c in range(CONV_WIDTH // LANES):
            zbuf_ref[c, SUBLANES + r0:SUBLANES + r0 + MIX_SUB, :] = z[:, c * LANES:(c + 1) * LANES]

        def delayed(k):
            start = SUBLANES - k + r0
            return jnp.concatenate([zbuf_ref[c, start:start + MIX_SUB, :]
                                    for c in range(CONV_WIDTH // LANES)], axis=1)

        conv = cw_ref[0:1, :] * delayed(2) + cw_ref[1:2, :] * delayed(1) + cw_ref[2:3, :] * z
        ya_in = (cb * conv).astype(_BF16)

        probs, dens = [], []
        for h in range(XATTN_HEADS):
            s = scores[h] * (XATTN_HEAD_DIM ** -0.5)
            p = jnp.exp(s - jnp.max(s, axis=-1, keepdims=True))
            dens.append(jnp.sum(p, axis=-1, keepdims=True))
            probs.append(p.astype(_BF16))
        return ya_in, probs, dens, gates

    def mix(r, ya_in, probs, dens, gates):
        rows = slice(r * MIX_SUB, (r + 1) * MIX_SUB)

        def gate(k):
            return jax.nn.sigmoid(gates[k] + bg_ref[:, k * D_MODEL:(k + 1) * D_MODEL])

        glu = _dot(ys_ref[rows, :], wglu_ref[...])
        y_a = _dot(ya_in, wco_ref[...])
        heads = []
        for h in range(XATTN_HEADS):
            vs = slice(XATTN_WIDTH + h * XATTN_HEAD_DIM, XATTN_WIDTH + (h + 1) * XATTN_HEAD_DIM)
            heads.append((_dot(probs[h], kv_ref[:, vs]) / dens[h]).astype(_BF16))
        y_c = _dot(jnp.concatenate(heads, axis=1), wxo_ref[...])

        merged = gate(0) * y_a + gate(2) * y_c
        merged += gate(1) * (glu[:, :D_MODEL] * jax.nn.sigmoid(glu[:, D_MODEL:]))
        v = ALPHA * x_ref[rows, :] + _dot(merged.astype(_BF16), wo_ref[...])
        o_ref[rows, :] = _layer_norm(v, g_ref[...], b_ref[...])

    n_chains = tm_ // MIX_SUB
    pending = project(0)
    for r in range(n_chains):
        current = pending
        if r + 1 < n_chains:
            pending = project(r + 1)
        mix(r, *current)

    zbuf_ref[:, 0:SUBLANES, :] = zbuf_ref[:, tm_:tm_ + SUBLANES, :]


def _layer_spec(param, layer):
    rest = param.shape[1:]
    if len(rest) == 1:
        return pl.BlockSpec((1,) + rest, lambda *_: (layer, 0), pipeline_mode=pl.Buffered(1))
    zeros = (0,) * len(rest)
    return pl.BlockSpec((None,) + rest, lambda *_: (layer,) + zeros,
                        pipeline_mode=pl.Buffered(1))


def _mixer_call(x2d, ys, mem2d, w_in, b_g, conv_w, w_co, w_glu, w_xo, w_o, w_kv, ln_g, ln_b,
                layer, w_cast, seq_len):
    n = x2d.shape[0]
    steps = n // MIX_TOKENS
    steps_per_seq = seq_len // MIX_TOKENS
    cast_rows, cast_cols = w_cast.shape[1:]
    cast_block = cast_rows // steps
    assert cast_block * steps == cast_rows and cast_block % (2 * SUBLANES) == 0
    small = (w_co, w_glu, w_xo, w_o, w_kv)
    operands = [w_in, b_g, conv_w, *small, ln_g, ln_b, w_cast]
    specs = [pl.BlockSpec(memory_space=pl.ANY), _layer_spec(b_g, layer),
             _layer_spec(conv_w, layer)]
    specs += [_layer_spec(w, layer) for w in small]
    specs += [_layer_spec(ln_g, layer), _layer_spec(ln_b, layer),
              pl.BlockSpec((None, cast_block, cast_cols), lambda i: (layer, i, 0))]
    proj_scratch = [pltpu.VMEM((D_MODEL, cols), _BF16)
                    for cols in (GATE_COLS, 3 * CONV_WIDTH, XATTN_WIDTH)]
    return pl.pallas_call(
        functools.partial(_mixer_kernel, steps_per_seq, layer),
        out_shape=(jax.ShapeDtypeStruct((n, D_MODEL), _F32),
                   jax.ShapeDtypeStruct((cast_rows, cast_cols), _BF16)),
        grid=(steps,),
        in_specs=[pl.BlockSpec((MIX_TOKENS, D_MODEL), lambda i: (i, 0)),
                  pl.BlockSpec((MIX_TOKENS, SSM_WIDTH), lambda i: (i, 0)),
                  pl.BlockSpec((MEM_LEN, D_MODEL), lambda i: (i // steps_per_seq, 0))]
                 + specs,
        out_specs=(pl.BlockSpec((MIX_TOKENS, D_MODEL), lambda i: (i, 0)),
                   pl.BlockSpec((cast_block, cast_cols), lambda i: (i, 0))),
        scratch_shapes=proj_scratch
                       + [pltpu.VMEM(w.shape[1:], _BF16) for w in small]
                       + [pltpu.VMEM((MEM_LEN, 2 * XATTN_WIDTH), _BF16),
                          pltpu.VMEM((2, D_MODEL, STAGE_COLS), _F32),
                          pltpu.SemaphoreType.DMA((2,)),
                          pltpu.VMEM((CONV_WIDTH // LANES, MIX_TOKENS + 2 * SUBLANES, LANES),
                                     _F32)],
        compiler_params=pltpu.CompilerParams(dimension_semantics=("arbitrary",),
                                             vmem_limit_bytes=VMEM_LIMIT),
        name="mixers",
    )(x2d, ys, mem2d, *operands)


def _mlp_kernel(x_ref, wu_ref, bu_ref, wd_ref, bd_ref, g_ref, b_ref, o_ref):
    n_chunks = D_FF // FF_CHUNK
    units = [(r, c) for r in range(MLP_TOKENS // MLP_SUB) for c in range(n_chunks)]
    xb, acc = {}, {}

    def up(r, c):
        if c == 0:
            xb[r] = x_ref[r * MLP_SUB:(r + 1) * MLP_SUB, :].astype(_BF16)
        return _dot(xb[r], wu_ref[:, c * FF_CHUNK:(c + 1) * FF_CHUNK])

    def down(r, c, pre):
        cols = slice(c * FF_CHUNK, (c + 1) * FF_CHUNK)
        h = jnp.maximum(pre + bu_ref[:, cols], 0.0)
        if c == 0:
            acc[r] = ALPHA * x_ref[r * MLP_SUB:(r + 1) * MLP_SUB, :] + bd_ref[...]
        acc[r] += _dot((h * h).astype(_BF16), wd_ref[cols, :])

    pre = up(*units[0])
    for k, (r, c) in enumerate(units):
        nxt = up(*units[k + 1]) if k + 1 < len(units) else None
        down(r, c, pre)
        if c == n_chunks - 1:
            o_ref[r * MLP_SUB:(r + 1) * MLP_SUB, :] = _layer_norm(acc.pop(r), g_ref[...],
                                                                  b_ref[...])
        pre = nxt


def _mlp_call(x1, w_up, b_up, w_down, b_down, ln_g, ln_b, layer):
    n = x1.shape[0]
    consts = [w_up, b_up, w_down, b_down, ln_g, ln_b]
    return pl.pallas_call(
        _mlp_kernel,
        out_shape=jax.ShapeDtypeStruct((n, D_MODEL), _F32),
        grid=(n // MLP_TOKENS,),
        in_specs=[pl.BlockSpec((MLP_TOKENS, D_MODEL), lambda i: (i, 0)),
                  _const_spec(w_up.shape), _layer_spec(b_up, layer),
                  _const_spec(w_down.shape), _layer_spec(b_down, layer),
                  _layer_spec(ln_g, layer), _layer_spec(ln_b, layer)],
        out_specs=pl.BlockSpec((MLP_TOKENS, D_MODEL), lambda i: (i, 0)),
        compiler_params=pltpu.CompilerParams(dimension_semantics=("arbitrary",),
                                             vmem_limit_bytes=VMEM_LIMIT),
        name="mlp",
    )(x1, *consts)


def kernel(x, mem, w_in, b_gate, conv_w, w_conv_out, ssm_lam_re, ssm_lam_im, ssm_log_dt,
           ssm_b_re, ssm_b_im, ssm_c_re, ssm_c_im, ssm_d, w_glu, w_kv, w_xattn_out, w_out,
           ln1_g, ln1_b, w_up, b_up, w_down, b_down, ln2_g, ln2_b):
    bsz, seq, d = x.shape
    assert d == D_MODEL and seq % (PACK * S5_ROWS) == 0 and seq % MIX_TOKENS == 0
    assert w_in.shape[0] == DEPTH == 1
    n = bsz * seq
    l = 0

    tables = _s5_prep(ssm_lam_re[l], ssm_lam_im[l], ssm_log_dt[l], ssm_b_re[l],
                      ssm_b_im[l], ssm_c_re[l], ssm_c_im[l])

    x2d = x.reshape(n, d)
    ys, w_up_b = _s5_call(x2d, w_in, tables, ssm_d, l, w_up, seq)
    x1, w_down_b = _mixer_call(x2d, ys, mem.reshape(bsz * MEM_LEN, d), w_in, b_gate, conv_w,
                               w_conv_out, w_glu, w_xattn_out, w_out, w_kv,
                               ln1_g, ln1_b, l, w_down, seq)
    out = _mlp_call(x1, w_up_b, b_up, w_down_b, b_down, ln2_g, ln2_b, l)
    return out.reshape(bsz, seq, d)
```

```python
import functools

import jax
import jax.numpy as jnp
from jax import lax
from jax.experimental import pallas as pl
from jax.experimental.pallas import tpu as pltpu

D_MODEL = 1024
MEM_LEN = 256
N_BRANCH = 3
CONV_WIDTH = 512
CONV_TAPS = 3
SSM_WIDTH = 512
SSM_GROUP = 16
SSM_GROUPS = 32
SSM_STATE = 64
XATTN_HEADS = 4
XATTN_HEAD_DIM = 128
XATTN_WIDTH = 512
D_FF = 4096
GATE_COLS = N_BRANCH * D_MODEL
DEPTH = 1
ALPHA = (2.0 * DEPTH) ** 0.25
LN_EPS = 1e-5

LANES = 128
SUBLANES = 8
PACK = SUBLANES
GROUPS_PER_TILE = LANES // SSM_GROUP
N_LANE_TILES = SSM_WIDTH // LANES
STATE_PER_TILE = GROUPS_PER_TILE * SSM_STATE
N_STATE = SSM_GROUPS * SSM_STATE
PACKED_TILE = PACK * LANES

S5_ROWS = 128
TM_PANEL = 256
MIX_TOKENS = 512
MIX_SUB = 256
STAGE_BLOCK = 512
MLP_TOKENS = 1024
MLP_SUB = 256
FF_CHUNK = 1024
VMEM_LIMIT = 60 * 1024 * 1024

_BF16 = jnp.bfloat16
_F32 = jnp.float32
TAP_PRECISION = lax.Precision.HIGHEST


def _const_spec(shape):
    zeros = (0,) * len(shape)
    return pl.BlockSpec(shape, lambda *_: zeros, pipeline_mode=pl.Buffered(1))


def _dot(a, b):
    return jnp.dot(a, b, preferred_element_type=_F32)


def _stage_cast(src_hbm, layer, pieces, stage_ref, sem):
    def copy(k):
        return pltpu.make_async_copy(src_hbm.at[(layer,) + pieces[k][2]],
                                     stage_ref.at[k % 2], sem.at[k % 2])

    copy(0).start()
    for k, (dst, dst_index, _) in enumerate(pieces):
        if k + 1 < len(pieces):
            copy(k + 1).start()
        copy(k).wait()
        dst[dst_index] = stage_ref[k % 2].astype(dst.dtype)


def _layer_norm(v, g, b):
    mu = jnp.mean(v, axis=-1, keepdims=True)
    c = v - mu
    var = jnp.mean(c * c, axis=-1, keepdims=True)
    return c * lax.rsqrt(var + LN_EPS) * g + b


def _complex_powers(ar, ai, n):
    out = [(jnp.ones_like(ar), jnp.zeros_like(ai))]
    for _ in range(n):
        pr, pi = out[-1]
        out.append((pr * ar - pi * ai, pr * ai + pi * ar))
    return out


def _build_tables(rowp_ref, colp_ref, bt_ref, ct_ref, w1_ref, tm_ref, w3_ref, a8_ref):
    row_pow = _complex_powers(rowp_ref[0:1, :], rowp_ref[1:2, :], PACK)
    col_pow = _complex_powers(colp_ref[:, 0:1], colp_ref[:, 1:2], PACK)
    zr, zi = rowp_ref[2:3, :], rowp_ref[3:4, :]
    a8_ref[0], a8_ref[1] = row_pow[PACK]
    rows1 = lax.broadcasted_iota(jnp.int32, (LANES, STATE_PER_TILE), 0) // SSM_GROUP
    cols1 = lax.broadcasted_iota(jnp.int32, (LANES, STATE_PER_TILE), 1) // SSM_STATE
    same1 = rows1 == cols1
    rows3 = lax.broadcasted_iota(jnp.int32, (STATE_PER_TILE, LANES), 0) // SSM_STATE
    cols3 = lax.broadcasted_iota(jnp.int32, (STATE_PER_TILE, LANES), 1) // SSM_GROUP
    same3 = rows3 == cols3

    lane_reps = STATE_PER_TILE // LANES
    btr = jnp.where(same1, jnp.concatenate([bt_ref[0]] * lane_reps, axis=1), 0.0)
    bti = jnp.where(same1, jnp.concatenate([bt_ref[1]] * lane_reps, axis=1), 0.0)
    bbar = None
    for j in range(PACK):
        qr, qi = row_pow[PACK - 1 - j]
        pr, pi = qr * zr - qi * zi, qr * zi + qi * zr
        sre = pr * btr - pi * bti
        sim = pr * bti + pi * btr
        w1_ref[j * LANES:(j + 1) * LANES, :STATE_PER_TILE] = sre.astype(w1_ref.dtype)
        w1_ref[j * LANES:(j + 1) * LANES, STATE_PER_TILE:] = sim.astype(w1_ref.dtype)
        if j == PACK - 1:
            bbar = jnp.concatenate([sre, sim], axis=1)

    ctr = jnp.where(same3, ct_ref[0], 0.0)
    cti = jnp.where(same3, ct_ref[1], 0.0)
    nctr = -ctr
    tker = []
    for q in range(PACK + 1):
        pr, pi = col_pow[q]
        slab = jnp.concatenate([ctr * pr - cti * pi, nctr * pi - cti * pr], axis=0)
        if q >= 1:
            w3_ref[:, (q - 1) * LANES:q * LANES] = slab.astype(w3_ref.dtype)
        if q < PACK:
            tker.append(jnp.dot(bbar, slab, precision=TAP_PRECISION,
                                preferred_element_type=_F32).astype(tm_ref.dtype))

    zero = jnp.zeros((LANES, LANES), tm_ref.dtype)
    for j in range(PACK):
        for i in range(PACK):
            tm_ref[j * LANES:(j + 1) * LANES, i * LANES:(i + 1) * LANES] = (
                tker[i - j] if i >= j else zero)


def _s5_prep(lam_re, lam_im, log_dt, b_re, b_im, c_re, c_im):
    lr, li = lam_re.astype(_F32), lam_im.astype(_F32)
    dt = jnp.exp(log_dt.astype(_F32))[:, None]
    mag = jnp.exp(lr * dt)
    ar = mag * jnp.cos(li * dt)
    ai = mag * jnp.sin(li * dt)
    den = lr * lr + li * li
    nr, ni = ar - 1.0, ai
    kr = (nr * lr + ni * li) / den
    ki = (ni * lr - nr * li) / den
    t, g = N_LANE_TILES, GROUPS_PER_TILE

    rowp = jnp.stack([ar, ai, kr, ki]).reshape(4, t, STATE_PER_TILE).transpose(1, 0, 2)
    colp = rowp.transpose(0, 2, 1)
    b = jnp.stack([b_re, b_im]).astype(_F32).reshape(2, t, g, SSM_STATE, SSM_GROUP)
    reps = LANES // SSM_STATE
    bt = jnp.broadcast_to(b.transpose(1, 0, 2, 4, 3)[:, :, :, :, None, :],
                          (t, 2, g, SSM_GROUP, reps, SSM_STATE)).reshape(t, 2, LANES, LANES)
    c = jnp.stack([c_re, c_im]).astype(_F32).reshape(2, t, g, SSM_GROUP, SSM_STATE)
    ct = jnp.broadcast_to(c.transpose(1, 0, 2, 4, 3)[:, :, :, :, None, :],
                          (t, 2, g, SSM_STATE, g, SSM_GROUP)).reshape(t, 2, STATE_PER_TILE, LANES)
    return rowp, colp, bt, ct


def _s5_kernel(steps_per_seq, x_ref, wu32_ref, rowp_ref, colp_ref, bt_ref, ct_ref, d_ref,
               o_ref, wu_ref, w1_ref, tm_ref, w3_ref, a8t_ref, a8_ref,
               u_ref, xp_ref, yi_ref, y_ref, zr_ref, zi_ref, sr_ref, si_ref, carry_ref):
    rows = x_ref.shape[0] // PACK

    @pl.when(pl.program_id(0) == 0)
    def _():
        wu_ref[...] = wu32_ref[...].astype(wu_ref.dtype)

        def build(t, c):
            _build_tables(rowp_ref.at[t], colp_ref.at[t], bt_ref.at[t], ct_ref.at[t],
                          w1_ref.at[t], tm_ref.at[t], w3_ref.at[t], a8t_ref.at[t])
            return c
        lax.fori_loop(0, N_LANE_TILES, build, 0)
        for t in range(N_LANE_TILES):
            a8_ref[:, :, t * STATE_PER_TILE:(t + 1) * STATE_PER_TILE] = a8t_ref[t]

    @pl.when(pl.program_id(0) % steps_per_seq == 0)
    def _():
        carry_ref[...] = jnp.zeros_like(carry_ref)

    u = _dot(x_ref[...].astype(_BF16), wu_ref[...])
    for t in range(N_LANE_TILES):
        u_ref[t] = u[:, t * LANES:(t + 1) * LANES]

    for t in range(N_LANE_TILES):
        xp_ref[t] = jnp.concatenate(
            [u_ref[t, pl.ds(j, rows, stride=PACK), :] for j in range(PACK)],
            axis=1).astype(_BF16)
        z = _dot(xp_ref[t], w1_ref[t])
        zr_ref[:, t * STATE_PER_TILE:(t + 1) * STATE_PER_TILE] = z[:, :STATE_PER_TILE]
        zi_ref[:, t * STATE_PER_TILE:(t + 1) * STATE_PER_TILE] = z[:, STATE_PER_TILE:]

    a_r = a8_ref[0]
    a_i = a8_ref[1]
    s_r = carry_ref[0]
    s_i = carry_ref[1]
    seg = rows // N_LANE_TILES
    for t in range(N_LANE_TILES):
        for c in range(PACKED_TILE // TM_PANEL):
            hi = (c + 1) * TM_PANEL
            yi_ref[t, :, c * TM_PANEL:hi] = _dot(xp_ref[t, :, :hi],
                                                 tm_ref[t, :hi, c * TM_PANEL:hi])
        for r in range(t * seg, (t + 1) * seg):
            sr_ref[r:r + 1, :] = s_r
            si_ref[r:r + 1, :] = s_i
            z_r = zr_ref[r:r + 1, :]
            z_i = zi_ref[r:r + 1, :]
            s_r, s_i = a_r * s_r - a_i * s_i + z_r, a_r * s_i + a_i * s_r + z_i
    carry_ref[0] = s_r
    carry_ref[1] = s_i

    for t in range(N_LANE_TILES):
        sl = slice(t * STATE_PER_TILE, (t + 1) * STATE_PER_TILE)
        sprev = jnp.concatenate([sr_ref[:, sl], si_ref[:, sl]], axis=1).astype(_BF16)
        y = yi_ref[t] + _dot(sprev, w3_ref[t])
        for i in range(PACK):
            y_ref[t, pl.ds(i, rows, stride=PACK), :] = y[:, i * LANES:(i + 1) * LANES]
        cols = slice(t * LANES, (t + 1) * LANES)
        o_ref[:, cols] = jax.nn.gelu(y_ref[t] + d_ref[:, cols] * u_ref[t]).astype(o_ref.dtype)


def _s5_call(x2d, w_in, tables, d_skip, layer, seq_len):
    rowp, colp, bt, ct = tables
    n = x2d.shape[0]
    tokens = S5_ROWS * PACK
    steps = n // tokens
    steps_per_seq = seq_len // tokens
    u_block = (GATE_COLS + 3 * CONV_WIDTH) // SSM_WIDTH
    table = pltpu.VMEM((N_LANE_TILES, PACKED_TILE, PACKED_TILE), _BF16)
    slab = pltpu.VMEM((N_LANE_TILES, tokens, LANES), _F32)
    state = pltpu.VMEM((S5_ROWS, N_STATE), _F32)
    return pl.pallas_call(
        functools.partial(_s5_kernel, steps_per_seq),
        out_shape=jax.ShapeDtypeStruct((n, SSM_WIDTH), _BF16),
        grid=(steps,),
        in_specs=[pl.BlockSpec((tokens, D_MODEL), lambda i: (i, 0)),
                  pl.BlockSpec((None, D_MODEL, SSM_WIDTH), lambda i: (layer, 0, u_block),
                               pipeline_mode=pl.Buffered(1))]
                 + [_const_spec(c.shape) for c in (rowp, colp, bt, ct)]
                 + [_layer_spec(d_skip, layer)],
        out_specs=pl.BlockSpec((tokens, SSM_WIDTH), lambda i: (i, 0)),
        scratch_shapes=[pltpu.VMEM((D_MODEL, SSM_WIDTH), _BF16), table, table, table,
                        pltpu.VMEM((N_LANE_TILES, 2, 1, STATE_PER_TILE), _F32),
                        pltpu.VMEM((2, 1, N_STATE), _F32), slab,
                        pltpu.VMEM((N_LANE_TILES, S5_ROWS, PACKED_TILE), _BF16),
                        pltpu.VMEM((N_LANE_TILES, S5_ROWS, PACKED_TILE), _F32),
                        slab, state, state, state, state,
                        pltpu.VMEM((2, 1, N_STATE), _F32)],
        compiler_params=pltpu.CompilerParams(dimension_semantics=("arbitrary",),
                                             vmem_limit_bytes=VMEM_LIMIT),
        name="s5",
    )(x2d, w_in, rowp, colp, bt, ct, d_skip)


def _mixer_kernel(steps_per_seq, layer, x_ref, ys_ref, mem_ref, win_hbm, bg_ref, cw_ref,
                  wco32_ref, wglu32_ref, wxo32_ref, wo32_ref, wkv32_ref, g_ref, b_ref,
                  o_ref,
                  wg_ref, wc_ref, wq_ref, wco_ref, wglu_ref, wxo_ref, wo_ref, wkv_ref, kv_ref,
                  stage_ref, stage_sem, zbuf_ref):
    tm_ = x_ref.shape[0]

    @pl.when(pl.program_id(0) == 0)
    def _():
        blocks = [(wg_ref, 0), (wc_ref, GATE_COLS),
                  (wq_ref, GATE_COLS + 3 * CONV_WIDTH + SSM_WIDTH)]
        every = slice(None)
        pieces = [(dst, (every, pl.ds(c, STAGE_BLOCK)), (every, pl.ds(src0 + c, STAGE_BLOCK)))
                  for dst, src0 in blocks for c in range(0, dst.shape[1], STAGE_BLOCK)]
        _stage_cast(win_hbm, layer, pieces, stage_ref, stage_sem)

        for src, dst in ((wco32_ref, wco_ref), (wglu32_ref, wglu_ref), (wxo32_ref, wxo_ref),
                         (wo32_ref, wo_ref), (wkv32_ref, wkv_ref)):
            dst[...] = src[...].astype(dst.dtype)

    @pl.when(pl.program_id(0) % steps_per_seq == 0)
    def _():
        kv_ref[...] = _dot(mem_ref[...].astype(_BF16), wkv_ref[...]).astype(kv_ref.dtype)
        zbuf_ref[:, 0:SUBLANES, :] = jnp.zeros((CONV_WIDTH // LANES, SUBLANES, LANES), _F32)

    def project(r):
        r0 = r * MIX_SUB
        xb = x_ref[r0:r0 + MIX_SUB, :].astype(_BF16)
        q = _dot(xb, wq_ref[...]).astype(_BF16)
        cb, cc, ch = [_dot(xb, wc_ref[:, k * CONV_WIDTH:(k + 1) * CONV_WIDTH]) for k in range(3)]
        scores = []
        for h in range(XATTN_HEADS):
            hs = slice(h * XATTN_HEAD_DIM, (h + 1) * XATTN_HEAD_DIM)
            scores.append(lax.dot_general(q[:, hs], kv_ref[:, hs], (((1,), (1,)), ((), ())),
                                          preferred_element_type=_F32))
        gates = [_dot(xb, wg_ref[:, k * D_MODEL:(k + 1) * D_MODEL]) for k in range(N_BRANCH)]

        z = cc * ch
        for c in range(CONV_WIDTH // LANES):
            zbuf_ref[c, SUBLANES + r0:SUBLANES + r0 + MIX_SUB, :] = z[:, c * LANES:(c + 1) * LANES]

        def delayed(k):
            start = SUBLANES - k + r0
            return jnp.concatenate([zbuf_ref[c, start:start + MIX_SUB, :]
                                    for c in range(CONV_WIDTH // LANES)], axis=1)

        conv = cw_ref[0:1, :] * delayed(2) + cw_ref[1:2, :] * delayed(1) + cw_ref[2:3, :] * z
        ya_in = (cb * conv).astype(_BF16)

        probs, dens = [], []
        for h in range(XATTN_HEADS):
            s = scores[h] * (XATTN_HEAD_DIM ** -0.5)
            p = jnp.exp(s - jnp.max(s, axis=-1, keepdims=True))
            dens.append(jnp.sum(p, axis=-1, keepdims=True))
            probs.append(p.astype(_BF16))
        return ya_in, probs, dens, gates

    def mix(r, ya_in, probs, dens, gates):
        rows = slice(r * MIX_SUB, (r + 1) * MIX_SUB)

        def gate(k):
            return jax.nn.sigmoid(gates[k] + bg_ref[:, k * D_MODEL:(k + 1) * D_MODEL])

        glu = _dot(ys_ref[rows, :], wglu_ref[...])
        y_a = _dot(ya_in, wco_ref[...])
        heads = []
        for h in range(XATTN_HEADS):
            vs = slice(XATTN_WIDTH + h * XATTN_HEAD_DIM, XATTN_WIDTH + (h + 1) * XATTN_HEAD_DIM)
            heads.append((_dot(probs[h], kv_ref[:, vs]) / dens[h]).astype(_BF16))
        y_c = _dot(jnp.concatenate(heads, axis=1), wxo_ref[...])

        merged = gate(0) * y_a + gate(2) * y_c
        merged += gate(1) * (glu[:, :D_MODEL] * jax.nn.sigmoid(glu[:, D_MODEL:]))
        v = ALPHA * x_ref[rows, :] + _dot(merged.astype(_BF16), wo_ref[...])
        o_ref[rows, :] = _layer_norm(v, g_ref[...], b_ref[...])

    n_chains = tm_ // MIX_SUB
    pending = project(0)
    for r in range(n_chains):
        current = pending
        if r + 1 < n_chains:
            pending = project(r + 1)
        mix(r, *current)

    zbuf_ref[:, 0:SUBLANES, :] = zbuf_ref[:, tm_:tm_ + SUBLANES, :]


def _layer_spec(param, layer):
    rest = param.shape[1:]
    if len(rest) == 1:
        return pl.BlockSpec((1,) + rest, lambda *_: (layer, 0), pipeline_mode=pl.Buffered(1))
    zeros = (0,) * len(rest)
    return pl.BlockSpec((None,) + rest, lambda *_: (layer,) + zeros,
                        pipeline_mode=pl.Buffered(1))


def _mixer_call(x2d, ys, mem2d, w_in, b_g, conv_w, w_co, w_glu, w_xo, w_o, w_kv, ln_g, ln_b,
                layer, seq_len):
    n = x2d.shape[0]
    steps = n // MIX_TOKENS
    steps_per_seq = seq_len // MIX_TOKENS
    small = (w_co, w_glu, w_xo, w_o, w_kv)
    operands = [w_in, b_g, conv_w, *small, ln_g, ln_b]
    specs = [pl.BlockSpec(memory_space=pl.ANY), _layer_spec(b_g, layer),
             _layer_spec(conv_w, layer)]
    specs += [_layer_spec(w, layer) for w in small]
    specs += [_layer_spec(ln_g, layer), _layer_spec(ln_b, layer)]
    proj_scratch = [pltpu.VMEM((D_MODEL, cols), _BF16)
                    for cols in (GATE_COLS, 3 * CONV_WIDTH, XATTN_WIDTH)]
    return pl.pallas_call(
        functools.partial(_mixer_kernel, steps_per_seq, layer),
        out_shape=jax.ShapeDtypeStruct((n, D_MODEL), _F32),
        grid=(steps,),
        in_specs=[pl.BlockSpec((MIX_TOKENS, D_MODEL), lambda i: (i, 0)),
                  pl.BlockSpec((MIX_TOKENS, SSM_WIDTH), lambda i: (i, 0)),
                  pl.BlockSpec((MEM_LEN, D_MODEL), lambda i: (i // steps_per_seq, 0))]
                 + specs,
        out_specs=pl.BlockSpec((MIX_TOKENS, D_MODEL), lambda i: (i, 0)),
        scratch_shapes=proj_scratch
                       + [pltpu.VMEM(w.shape[1:], _BF16) for w in small]
                       + [pltpu.VMEM((MEM_LEN, 2 * XATTN_WIDTH), _BF16),
                          pltpu.VMEM((2, D_MODEL, STAGE_BLOCK), _F32),
                          pltpu.SemaphoreType.DMA((2,)),
                          pltpu.VMEM((CONV_WIDTH // LANES, MIX_TOKENS + 2 * SUBLANES, LANES),
                                     _F32)],
        compiler_params=pltpu.CompilerParams(dimension_semantics=("arbitrary",),
                                             vmem_limit_bytes=VMEM_LIMIT),
        name="mixers",
    )(x2d, ys, mem2d, *operands)


def _mlp_kernel(layer, x_ref, wu_hbm, bu_ref, wd_hbm, bd_ref, g_ref, b_ref, o_ref,
                wu_ref, wd_ref, stage_u_ref, stage_d_ref, sem_u, sem_d):
    @pl.when(pl.program_id(0) == 0)
    def _():
        every = slice(None)
        blocks = [pl.ds(k, STAGE_BLOCK) for k in range(0, D_FF, STAGE_BLOCK)]
        _stage_cast(wu_hbm, layer, [(wu_ref, (every, blk), (every, blk)) for blk in blocks],
                    stage_u_ref, sem_u)
        _stage_cast(wd_hbm, layer, [(wd_ref, (blk, every), (blk, every)) for blk in blocks],
                    stage_d_ref, sem_d)

    n_chunks = D_FF // FF_CHUNK
    units = [(r, c) for r in range(MLP_TOKENS // MLP_SUB) for c in range(n_chunks)]
    xb, acc = {}, {}

    def up(r, c):
        if c == 0:
            xb[r] = x_ref[r * MLP_SUB:(r + 1) * MLP_SUB, :].astype(_BF16)
        return _dot(xb[r], wu_ref[:, c * FF_CHUNK:(c + 1) * FF_CHUNK])

    def down(r, c, pre):
        cols = slice(c * FF_CHUNK, (c + 1) * FF_CHUNK)
        h = jnp.maximum(pre + bu_ref[:, cols], 0.0)
        if c == 0:
            acc[r] = ALPHA * x_ref[r * MLP_SUB:(r + 1) * MLP_SUB, :] + bd_ref[...]
        acc[r] += _dot((h * h).astype(_BF16), wd_ref[cols, :])

    pre = up(*units[0])
    for k, (r, c) in enumerate(units):
        nxt = up(*units[k + 1]) if k + 1 < len(units) else None
        down(r, c, pre)
        if c == n_chunks - 1:
            o_ref[r * MLP_SUB:(r + 1) * MLP_SUB, :] = _layer_norm(acc.pop(r), g_ref[...],
                                                                  b_ref[...])
        pre = nxt


def _mlp_call(x1, w_up, b_up, w_down, b_down, ln_g, ln_b, layer):
    n = x1.shape[0]
    consts = [w_up, b_up, w_down, b_down, ln_g, ln_b]
    return pl.pallas_call(
        functools.partial(_mlp_kernel, layer),
        out_shape=jax.ShapeDtypeStruct((n, D_MODEL), _F32),
        grid=(n // MLP_TOKENS,),
        in_specs=[pl.BlockSpec((MLP_TOKENS, D_MODEL), lambda i: (i, 0)),
                  pl.BlockSpec(memory_space=pl.ANY), _layer_spec(b_up, layer),
                  pl.BlockSpec(memory_space=pl.ANY), _layer_spec(b_down, layer),
                  _layer_spec(ln_g, layer), _layer_spec(ln_b, layer)],
        out_specs=pl.BlockSpec((MLP_TOKENS, D_MODEL), lambda i: (i, 0)),
        scratch_shapes=[pltpu.VMEM((D_MODEL, D_FF), _BF16), pltpu.VMEM((D_FF, D_MODEL), _BF16),
                        pltpu.VMEM((2, D_MODEL, STAGE_BLOCK), _F32),
                        pltpu.VMEM((2, STAGE_BLOCK, D_MODEL), _F32),
                        pltpu.SemaphoreType.DMA((2,)), pltpu.SemaphoreType.DMA((2,))],
        compiler_params=pltpu.CompilerParams(dimension_semantics=("arbitrary",),
                                             vmem_limit_bytes=VMEM_LIMIT),
        name="mlp",
    )(x1, *consts)


def kernel(x, mem, w_in, b_gate, conv_w, w_conv_out, ssm_lam_re, ssm_lam_im, ssm_log_dt,
           ssm_b_re, ssm_b_im, ssm_c_re, ssm_c_im, ssm_d, w_glu, w_kv, w_xattn_out, w_out,
           ln1_g, ln1_b, w_up, b_up, w_down, b_down, ln2_g, ln2_b):
    bsz, seq, d = x.shape
    assert d == D_MODEL and seq % (PACK * S5_ROWS) == 0 and seq % MIX_TOKENS == 0
    assert w_in.shape[0] == DEPTH == 1
    n = bsz * seq
    l = 0

    tables = _s5_prep(ssm_lam_re[l], ssm_lam_im[l], ssm_log_dt[l], ssm_b_re[l],
                      ssm_b_im[l], ssm_c_re[l], ssm_c_im[l])

    x2d = x.reshape(n, d)
    ys = _s5_call(x2d, w_in, tables, ssm_d, l, seq)
    x1 = _mixer_call(x2d, ys, mem.reshape(bsz * MEM_LEN, d), w_in, b_gate, conv_w,
                     w_conv_out, w_glu, w_xattn_out, w_out, w_kv, ln1_g, ln1_b, l, seq)
    out = _mlp_call(x1, w_up, b_up, w_down, b_down, ln2_g, ln2_b, l)
    return out.reshape(bsz, seq, d)
```

```python
import functools

import jax
import jax.numpy as jnp
from jax import lax
from jax.experimental import pallas as pl
from jax.experimental.pallas import tpu as pltpu

D_MODEL = 1024
MEM_LEN = 256
N_BRANCH = 3
CONV_WIDTH = 512
CONV_TAPS = 3
SSM_WIDTH = 512
SSM_GROUP = 16
SSM_GROUPS = 32
SSM_STATE = 64
XATTN_HEADS = 4
XATTN_HEAD_DIM = 128
XATTN_WIDTH = 512
D_FF = 4096
GATE_COLS = N_BRANCH * D_MODEL
DEPTH = 1
ALPHA = (2.0 * DEPTH) ** 0.25
LN_EPS = 1e-5

LANES = 128
SUBLANES = 8
PACK = SUBLANES
GROUPS_PER_TILE = LANES // SSM_GROUP
N_LANE_TILES = SSM_WIDTH // LANES
STATE_PER_TILE = GROUPS_PER_TILE * SSM_STATE
N_STATE = SSM_GROUPS * SSM_STATE
PACKED_TILE = PACK * LANES

S5_ROWS = 128
TM_PANEL = 256
MIX_TOKENS = 512
MIX_SUB = 256
STAGE_COLS = 512
MLP_TOKENS = 1024
MLP_SUB = 256
FF_CHUNK = 1024
VMEM_LIMIT = 60 * 1024 * 1024

_BF16 = jnp.bfloat16
_F32 = jnp.float32


def _const_spec(shape):
    zeros = (0,) * len(shape)
    return pl.BlockSpec(shape, lambda *_: zeros, pipeline_mode=pl.Buffered(1))


def _dot(a, b):
    return jnp.dot(a, b, preferred_element_type=_F32)


def _layer_norm(v, g, b):
    mu = jnp.mean(v, axis=-1, keepdims=True)
    c = v - mu
    var = jnp.mean(c * c, axis=-1, keepdims=True)
    return c * lax.rsqrt(var + LN_EPS) * g + b


def _complex_powers(ar, ai, n):
    out = [(jnp.ones_like(ar), jnp.zeros_like(ai))]
    for _ in range(n):
        pr, pi = out[-1]
        out.append((pr * ar - pi * ai, pr * ai + pi * ar))
    return out


def _build_tables(rowp_ref, colp_ref, bt_ref, ct_ref, w1_ref, tm_ref, w3_ref, a8_ref):
    row_pow = _complex_powers(rowp_ref[0:1, :], rowp_ref[1:2, :], PACK)
    zr, zi = rowp_ref[2:3, :], rowp_ref[3:4, :]
    a8_ref[0], a8_ref[1] = row_pow[PACK]
    rows1 = lax.broadcasted_iota(jnp.int32, (LANES, STATE_PER_TILE), 0) // SSM_GROUP
    cols1 = lax.broadcasted_iota(jnp.int32, (LANES, STATE_PER_TILE), 1) // SSM_STATE
    same1 = rows1 == cols1
    rows3 = lax.broadcasted_iota(jnp.int32, (STATE_PER_TILE, LANES), 0) // SSM_STATE
    cols3 = lax.broadcasted_iota(jnp.int32, (STATE_PER_TILE, LANES), 1) // SSM_GROUP
    same3 = rows3 == cols3

    lane_reps = STATE_PER_TILE // LANES
    btr = jnp.where(same1, jnp.concatenate([bt_ref[0]] * lane_reps, axis=1), 0.0)
    bti = jnp.where(same1, jnp.concatenate([bt_ref[1]] * lane_reps, axis=1), 0.0)
    for j in range(PACK):
        qr, qi = row_pow[PACK - 1 - j]
        pr, pi = qr * zr - qi * zi, qr * zi + qi * zr
        sre = pr * btr - pi * bti
        sim = pr * bti + pi * btr
        w1_ref[j * LANES:(j + 1) * LANES, :STATE_PER_TILE] = sre.astype(w1_ref.dtype)
        w1_ref[j * LANES:(j + 1) * LANES, STATE_PER_TILE:] = sim.astype(w1_ref.dtype)
    bbar = w1_ref[(PACK - 1) * LANES:PACK * LANES, :]

    ctr = jnp.where(same3, ct_ref[0], 0.0)
    cti = jnp.where(same3, ct_ref[1], 0.0)
    ar = jnp.broadcast_to(colp_ref[:, 0:1], (STATE_PER_TILE, LANES))
    ai = jnp.broadcast_to(colp_ref[:, 1:2], (STATE_PER_TILE, LANES))
    er, ei = ctr, cti
    tker = []
    for q in range(PACK + 1):
        slab = jnp.concatenate([er, -ei], axis=0).astype(w3_ref.dtype)
        if q >= 1:
            w3_ref[:, (q - 1) * LANES:q * LANES] = slab
        if q < PACK:
            tker.append(_dot(bbar, slab).astype(tm_ref.dtype))
            er, ei = er * ar - ei * ai, er * ai + ei * ar

    zero = jnp.zeros((LANES, LANES), tm_ref.dtype)
    for j in range(PACK):
        for i in range(PACK):
            tm_ref[j * LANES:(j + 1) * LANES, i * LANES:(i + 1) * LANES] = (
                tker[i - j] if i >= j else zero)


def _s5_prep(lam_re, lam_im, log_dt, b_re, b_im, c_re, c_im):
    lr, li = lam_re.astype(_F32), lam_im.astype(_F32)
    dt = jnp.exp(log_dt.astype(_F32))[:, None]
    mag = jnp.exp(lr * dt)
    ar = mag * jnp.cos(li * dt)
    ai = mag * jnp.sin(li * dt)
    den = lr * lr + li * li
    nr, ni = ar - 1.0, ai
    kr = (nr * lr + ni * li) / den
    ki = (ni * lr - nr * li) / den
    t, g = N_LANE_TILES, GROUPS_PER_TILE

    rowp = jnp.stack([ar, ai, kr, ki]).reshape(4, t, STATE_PER_TILE).transpose(1, 0, 2)
    colp = rowp.transpose(0, 2, 1)
    b = jnp.stack([b_re, b_im]).astype(_F32).reshape(2, t, g, SSM_STATE, SSM_GROUP)
    reps = LANES // SSM_STATE
    bt = jnp.broadcast_to(b.transpose(1, 0, 2, 4, 3)[:, :, :, :, None, :],
                          (t, 2, g, SSM_GROUP, reps, SSM_STATE)).reshape(t, 2, LANES, LANES)
    c = jnp.stack([c_re, c_im]).astype(_F32).reshape(2, t, g, SSM_GROUP, SSM_STATE)
    ct = jnp.broadcast_to(c.transpose(1, 0, 2, 4, 3)[:, :, :, :, None, :],
                          (t, 2, g, SSM_STATE, g, SSM_GROUP)).reshape(t, 2, STATE_PER_TILE, LANES)
    return rowp, colp, bt, ct


def _s5_kernel(steps_per_seq, x_ref, wu32_ref, rowp_ref, colp_ref, bt_ref, ct_ref, d_ref, cast_ref,
               o_ref, cast_out_ref, wu_ref, w1_ref, tm_ref, w3_ref, a8t_ref, a8_ref,
               u_ref, xp_ref, yi_ref, y_ref, zr_ref, zi_ref, sr_ref, si_ref, carry_ref):
    rows = x_ref.shape[0] // PACK

    cast_out_ref[...] = cast_ref[...].astype(cast_out_ref.dtype)

    @pl.when(pl.program_id(0) == 0)
    def _():
        wu_ref[...] = wu32_ref[...].astype(wu_ref.dtype)

        def build(t, c):
            _build_tables(rowp_ref.at[t], colp_ref.at[t], bt_ref.at[t], ct_ref.at[t],
                          w1_ref.at[t], tm_ref.at[t], w3_ref.at[t], a8t_ref.at[t])
            return c
        lax.fori_loop(0, N_LANE_TILES, build, 0)
        for t in range(N_LANE_TILES):
            a8_ref[:, :, t * STATE_PER_TILE:(t + 1) * STATE_PER_TILE] = a8t_ref[t]

    @pl.when(pl.program_id(0) % steps_per_seq == 0)
    def _():
        carry_ref[...] = jnp.zeros_like(carry_ref)

    u = _dot(x_ref[...].astype(_BF16), wu_ref[...])
    for t in range(N_LANE_TILES):
        u_ref[t] = u[:, t * LANES:(t + 1) * LANES]

    for t in range(N_LANE_TILES):
        xp_ref[t] = jnp.concatenate(
            [u_ref[t, pl.ds(j, rows, stride=PACK), :] for j in range(PACK)],
            axis=1).astype(_BF16)
        z = _dot(xp_ref[t], w1_ref[t])
        zr_ref[:, t * STATE_PER_TILE:(t + 1) * STATE_PER_TILE] = z[:, :STATE_PER_TILE]
        zi_ref[:, t * STATE_PER_TILE:(t + 1) * STATE_PER_TILE] = z[:, STATE_PER_TILE:]

    a_r = a8_ref[0]
    a_i = a8_ref[1]
    s_r = carry_ref[0]
    s_i = carry_ref[1]
    seg = rows // N_LANE_TILES
    for t in range(N_LANE_TILES):
        for c in range(PACKED_TILE // TM_PANEL):
            hi = (c + 1) * TM_PANEL
            yi_ref[t, :, c * TM_PANEL:hi] = _dot(xp_ref[t, :, :hi],
                                                 tm_ref[t, :hi, c * TM_PANEL:hi])
        for r in range(t * seg, (t + 1) * seg):
            sr_ref[r:r + 1, :] = s_r
            si_ref[r:r + 1, :] = s_i
            z_r = zr_ref[r:r + 1, :]
            z_i = zi_ref[r:r + 1, :]
            s_r, s_i = a_r * s_r - a_i * s_i + z_r, a_r * s_i + a_i * s_r + z_i
    carry_ref[0] = s_r
    carry_ref[1] = s_i

    for t in range(N_LANE_TILES):
        sl = slice(t * STATE_PER_TILE, (t + 1) * STATE_PER_TILE)
        sprev = jnp.concatenate([sr_ref[:, sl], si_ref[:, sl]], axis=1).astype(_BF16)
        y = yi_ref[t] + _dot(sprev, w3_ref[t])
        for i in range(PACK):
            y_ref[t, pl.ds(i, rows, stride=PACK), :] = y[:, i * LANES:(i + 1) * LANES]
        cols = slice(t * LANES, (t + 1) * LANES)
        o_ref[:, cols] = jax.nn.gelu(y_ref[t] + d_ref[:, cols] * u_ref[t]).astype(o_ref.dtype)


def _s5_call(x2d, w_in, tables, d_skip, layer, w_cast, seq_len):
    rowp, colp, bt, ct = tables
    n = x2d.shape[0]
    tokens = S5_ROWS * PACK
    steps = n // tokens
    steps_per_seq = seq_len // tokens
    u_block = (GATE_COLS + 3 * CONV_WIDTH) // SSM_WIDTH
    cast_rows, cast_cols = w_cast.shape[1:]
    cast_block = cast_cols // steps
    assert cast_block * steps == cast_cols and cast_block % LANES == 0
    table = pltpu.VMEM((N_LANE_TILES, PACKED_TILE, PACKED_TILE), _BF16)
    slab = pltpu.VMEM((N_LANE_TILES, tokens, LANES), _F32)
    state = pltpu.VMEM((S5_ROWS, N_STATE), _F32)
    return pl.pallas_call(
        functools.partial(_s5_kernel, steps_per_seq),
        out_shape=(jax.ShapeDtypeStruct((n, SSM_WIDTH), _BF16),
                   jax.ShapeDtypeStruct((cast_rows, cast_cols), _BF16)),
        grid=(steps,),
        in_specs=[pl.BlockSpec((tokens, D_MODEL), lambda i: (i, 0)),
                  pl.BlockSpec((None, D_MODEL, SSM_WIDTH), lambda i: (layer, 0, u_block),
                               pipeline_mode=pl.Buffered(1))]
                 + [_const_spec(c.shape) for c in (rowp, colp, bt, ct)]
                 + [_layer_spec(d_skip, layer),
                    pl.BlockSpec((None, cast_rows, cast_block), lambda i: (layer, 0, i))],
        out_specs=(pl.BlockSpec((tokens, SSM_WIDTH), lambda i: (i, 0)),
                   pl.BlockSpec((cast_rows, cast_block), lambda i: (0, i))),
        scratch_shapes=[pltpu.VMEM((D_MODEL, SSM_WIDTH), _BF16), table, table, table,
                        pltpu.VMEM((N_LANE_TILES, 2, 1, STATE_PER_TILE), _F32),
                        pltpu.VMEM((2, 1, N_STATE), _F32), slab,
                        pltpu.VMEM((N_LANE_TILES, S5_ROWS, PACKED_TILE), _BF16),
                        pltpu.VMEM((N_LANE_TILES, S5_ROWS, PACKED_TILE), _F32),
                        slab, state, state, state, state,
                        pltpu.VMEM((2, 1, N_STATE), _F32)],
        compiler_params=pltpu.CompilerParams(dimension_semantics=("arbitrary",),
                                             vmem_limit_bytes=VMEM_LIMIT),
        name="s5",
    )(x2d, w_in, rowp, colp, bt, ct, d_skip, w_cast)


def _mixer_kernel(steps_per_seq, layer, x_ref, ys_ref, mem_ref, win_hbm, bg_ref, cw_ref,
                  wco32_ref, wglu32_ref, wxo32_ref, wo32_ref, wkv32_ref, g_ref, b_ref,
                  cast_ref, o_ref, cast_out_ref,
                  wg_ref, wc_ref, wq_ref, wco_ref, wglu_ref, wxo_ref, wo_ref, wkv_ref, kv_ref,
                  stage_ref, stage_sem, zbuf_ref):
    tm_ = x_ref.shape[0]

    cast_out_ref[...] = cast_ref[...].astype(cast_out_ref.dtype)

    @pl.when(pl.program_id(0) == 0)
    def _():
        blocks = [(wg_ref, 0), (wc_ref, GATE_COLS),
                  (wq_ref, GATE_COLS + 3 * CONV_WIDTH + SSM_WIDTH)]
        pieces = [(dst, c, src0 + c) for dst, src0 in blocks
                  for c in range(0, dst.shape[1], STAGE_COLS)]

        def copy(k):
            return pltpu.make_async_copy(
                win_hbm.at[layer, :, pl.ds(pieces[k][2], STAGE_COLS)],
                stage_ref.at[k % 2], stage_sem.at[k % 2])

        copy(0).start()
        for k, (dst, c, _) in enumerate(pieces):
            if k + 1 < len(pieces):
                copy(k + 1).start()
            copy(k).wait()
            dst[:, c:c + STAGE_COLS] = stage_ref[k % 2].astype(dst.dtype)

        for src, dst in ((wco32_ref, wco_ref), (wglu32_ref, wglu_ref), (wxo32_ref, wxo_ref),
                         (wo32_ref, wo_ref), (wkv32_ref, wkv_ref)):
            dst[...] = src[...].astype(dst.dtype)

    @pl.when(pl.program_id(0) % steps_per_seq == 0)
    def _():
        kv_ref[...] = _dot(mem_ref[...].astype(_BF16), wkv_ref[...]).astype(kv_ref.dtype)
        zbuf_ref[:, 0:SUBLANES, :] = jnp.zeros((CONV_WIDTH // LANES, SUBLANES, LANES), _F32)

    def project(r):
        r0 = r * MIX_SUB
        xb = x_ref[r0:r0 + MIX_SUB, :].astype(_BF16)
        q = _dot(xb, wq_ref[...]).astype(_BF16)
        cb, cc, ch = [_dot(xb, wc_ref[:, k * CONV_WIDTH:(k + 1) * CONV_WIDTH]) for k in range(3)]
        scores = []
        for h in range(XATTN_HEADS):
            hs = slice(h * XATTN_HEAD_DIM, (h + 1) * XATTN_HEAD_DIM)
            scores.append(lax.dot_general(q[:, hs], kv_ref[:, hs], (((1,), (1,)), ((), ())),
                                          preferred_element_type=_F32))
        gates = [_dot(xb, wg_ref[:, k * D_MODEL:(k + 1) * D_MODEL]) for k in range(N_BRANCH)]

        z = cc * ch
        for c in range(CONV_WIDTH // LANES):
            zbuf_ref[c, SUBLANES + r0:SUBLANES + r0 + MIX_SUB, :] = z[:, c * LANES:(c + 1) * LANES]

        def delayed(k):
            start = SUBLANES - k + r0
            return jnp.concatenate([zbuf_ref[c, start:start + MIX_SUB, :]
                                    for c in range(CONV_WIDTH // LANES)], axis=1)

        conv = cw_ref[0:1, :] * delayed(2) + cw_ref[1:2, :] * delayed(1) + cw_ref[2:3, :] * z
        ya_in = (cb * conv).astype(_BF16)

        probs, dens = [], []
        for h in range(XATTN_HEADS):
            s = scores[h] * (XATTN_HEAD_DIM ** -0.5)
            p = jnp.exp(s - jnp.max(s, axis=-1, keepdims=True))
            dens.append(jnp.sum(p, axis=-1, keepdims=True))
            probs.append(p.astype(_BF16))
        return ya_in, probs, dens, gates

    def mix(r, ya_in, probs, dens, gates):
        rows = slice(r * MIX_SUB, (r + 1) * MIX_SUB)

        def gate(k):
            return jax.nn.sigmoid(gates[k] + bg_ref[:, k * D_MODEL:(k + 1) * D_MODEL])

        glu = _dot(ys_ref[rows, :], wglu_ref[...])
        y_a = _dot(ya_in, wco_ref[...])
        heads = []
        for h in range(XATTN_HEADS):
            vs = slice(XATTN_WIDTH + h * XATTN_HEAD_DIM, XATTN_WIDTH + (h + 1) * XATTN_HEAD_DIM)
            heads.append((_dot(probs[h], kv_ref[:, vs]) / dens[h]).astype(_BF16))
        y_c = _dot(jnp.concatenate(heads, axis=1), wxo_ref[...])

        merged = gate(0) * y_a + gate(2) * y_c
        merged += gate(1) * (glu[:, :D_MODEL] * jax.nn.sigmoid(glu[:, D_MODEL:]))
        v = ALPHA * x_ref[rows, :] + _dot(merged.astype(_BF16), wo_ref[...])
        o_ref[rows, :] = _layer_norm(v, g_ref[...], b_ref[...])

    n_chains = tm_ // MIX_SUB
    pending = project(0)
    for r in range(n_chains):
        current = pending
        if r + 1 < n_chains:
            pending = project(r + 1)
        mix(r, *current)

    zbuf_ref[:, 0:SUBLANES, :] = zbuf_ref[:, tm_:tm_ + SUBLANES, :]


def _layer_spec(param, layer):
    rest = param.shape[1:]
    if len(rest) == 1:
        return pl.BlockSpec((1,) + rest, lambda *_: (layer, 0), pipeline_mode=pl.Buffered(1))
    zeros = (0,) * len(rest)
    return pl.BlockSpec((None,) + rest, lambda *_: (layer,) + zeros,
                        pipeline_mode=pl.Buffered(1))


def _mixer_call(x2d, ys, mem2d, w_in, b_g, conv_w, w_co, w_glu, w_xo, w_o, w_kv, ln_g, ln_b,
                layer, w_cast, seq_len):
    n = x2d.shape[0]
    steps = n // MIX_TOKENS
    steps_per_seq = seq_len // MIX_TOKENS
    cast_rows, cast_cols = w_cast.shape[1:]
    cast_block = cast_rows // steps
    assert cast_block * steps == cast_rows and cast_block % (2 * SUBLANES) == 0
    small = (w_co, w_glu, w_xo, w_o, w_kv)
    operands = [w_in, b_g, conv_w, *small, ln_g, ln_b, w_cast]
    specs = [pl.BlockSpec(memory_space=pl.ANY), _layer_spec(b_g, layer),
             _layer_spec(conv_w, layer)]
    specs += [_layer_spec(w, layer) for w in small]
    specs += [_layer_spec(ln_g, layer), _layer_spec(ln_b, layer),
              pl.BlockSpec((None, cast_block, cast_cols), lambda i: (layer, i, 0))]
    proj_scratch = [pltpu.VMEM((D_MODEL, cols), _BF16)
                    for cols in (GATE_COLS, 3 * CONV_WIDTH, XATTN_WIDTH)]
    return pl.pallas_call(
        functools.partial(_mixer_kernel, steps_per_seq, layer),
        out_shape=(jax.ShapeDtypeStruct((n, D_MODEL), _F32),
                   jax.ShapeDtypeStruct((cast_rows, cast_cols), _BF16)),
        grid=(steps,),
        in_specs=[pl.BlockSpec((MIX_TOKENS, D_MODEL), lambda i: (i, 0)),
                  pl.BlockSpec((MIX_TOKENS, SSM_WIDTH), lambda i: (i, 0)),
                  pl.BlockSpec((MEM_LEN, D_MODEL), lambda i: (i // steps_per_seq, 0))]
                 + specs,
        out_specs=(pl.BlockSpec((MIX_TOKENS, D_MODEL), lambda i: (i, 0)),
                   pl.BlockSpec((cast_block, cast_cols), lambda i: (i, 0))),
        scratch_shapes=proj_scratch
                       + [pltpu.VMEM(w.shape[1:], _BF16) for w in small]
                       + [pltpu.VMEM((MEM_LEN, 2 * XATTN_WIDTH), _BF16),
                          pltpu.VMEM((2, D_MODEL, STAGE_COLS), _F32),
                          pltpu.SemaphoreType.DMA((2,)),
                          pltpu.VMEM((CONV_WIDTH // LANES, MIX_TOKENS + 2 * SUBLANES, LANES),
                                     _F32)],
        compiler_params=pltpu.CompilerParams(dimension_semantics=("arbitrary",),
                                             vmem_limit_bytes=VMEM_LIMIT),
        name="mixers",
    )(x2d, ys, mem2d, *operands)


def _mlp_kernel(x_ref, wu_ref, bu_ref, wd_ref, bd_ref, g_ref, b_ref, o_ref, acc_ref):
    n_chunks = D_FF // FF_CHUNK
    xb = x_ref[...].astype(_BF16)

    def up(c):
        return _dot(xb, wu_ref[:, c * FF_CHUNK:(c + 1) * FF_CHUNK])

    pre = up(0)
    for c in range(n_chunks):
        cols = slice(c * FF_CHUNK, (c + 1) * FF_CHUNK)
        nxt = up(c + 1) if c + 1 < n_chunks else None
        h = jnp.maximum(pre + bu_ref[:, cols], 0.0)
        hb = (h * h).astype(_BF16)
        if c == 0:
            acc_ref[...] = ALPHA * x_ref[...] + bd_ref[...]
        if c + 1 < n_chunks:
            acc_ref[...] += _dot(hb, wd_ref[cols, :])
        else:
            for r in range(MLP_TOKENS // MLP_SUB):
                rows = slice(r * MLP_SUB, (r + 1) * MLP_SUB)
                v = acc_ref[rows, :] + _dot(hb[rows, :], wd_ref[cols, :])
                o_ref[rows, :] = _layer_norm(v, g_ref[...], b_ref[...])
        pre = nxt


def _mlp_call(x1, w_up, b_up, w_down, b_down, ln_g, ln_b, layer):
    n = x1.shape[0]
    consts = [w_up, b_up, w_down, b_down, ln_g, ln_b]
    return pl.pallas_call(
        _mlp_kernel,
        out_shape=jax.ShapeDtypeStruct((n, D_MODEL), _F32),
        grid=(n // MLP_TOKENS,),
        in_specs=[pl.BlockSpec((MLP_TOKENS, D_MODEL), lambda i: (i, 0)),
                  _const_spec(w_up.shape), _layer_spec(b_up, layer),
                  _const_spec(w_down.shape), _layer_spec(b_down, layer),
                  _layer_spec(ln_g, layer), _layer_spec(ln_b, layer)],
        out_specs=pl.BlockSpec((MLP_TOKENS, D_MODEL), lambda i: (i, 0)),
        scratch_shapes=[pltpu.VMEM((MLP_TOKENS, D_MODEL), _F32)],
        compiler_params=pltpu.CompilerParams(dimension_semantics=("arbitrary",),
                                             vmem_limit_bytes=VMEM_LIMIT),
        name="mlp",
    )(x1, *consts)


def kernel(x, mem, w_in, b_gate, conv_w, w_conv_out, ssm_lam_re, ssm_lam_im, ssm_log_dt,
           ssm_b_re, ssm_b_im, ssm_c_re, ssm_c_im, ssm_d, w_glu, w_kv, w_xattn_out, w_out,
           ln1_g, ln1_b, w_up, b_up, w_down, b_down, ln2_g, ln2_b):
    bsz, seq, d = x.shape
    assert d == D_MODEL and seq % (PACK * S5_ROWS) == 0 and seq % MIX_TOKENS == 0
    assert w_in.shape[0] == DEPTH == 1
    n = bsz * seq
    l = 0

    tables = _s5_prep(ssm_lam_re[l], ssm_lam_im[l], ssm_log_dt[l], ssm_b_re[l],
                      ssm_b_im[l], ssm_c_re[l], ssm_c_im[l])

    x2d = x.reshape(n, d)
    ys, w_up_b = _s5_call(x2d, w_in, tables, ssm_d, l, w_up, seq)
    x1, w_down_b = _mixer_call(x2d, ys, mem.reshape(bsz * MEM_LEN, d), w_in, b_gate, conv_w,
                               w_conv_out, w_glu, w_xattn_out, w_out, w_kv,
                               ln1_g, ln1_b, l, w_down, seq)
    out = _mlp_call(x1, w_up_b, b_up, w_down_b, b_down, ln2_g, ln2_b, l)
    return out.reshape(bsz, seq, d)
```
